```python
import jax, jax.numpy as jnp
from jax import lax
import numpy as np


D_MODEL = 1024
BATCH = 8
SEQ = 4096
DEPTH = 4

CHUNK = 64
EPS = 1e-6
LB_FLOOR = 1e-30
N_BRANCH = 4
BRANCH_W = D_MODEL // 2

GLA_HEADS = 4
GLA_DK = 64
GLA_DV = BRANCH_W // GLA_HEADS
GLA_RANK = 16
GLA_TAU = 16.0
HG_HEADS = 4
HG_DK = 64
HG_DV = BRANCH_W // HG_HEADS
SSM_HEADS = 8
SSM_HEAD_DIM = BRANCH_W // SSM_HEADS
SSM_GROUPS = 2
SSM_STATE = 64
SSM_CONV = 4
SSM_INNER = SSM_HEADS * SSM_HEAD_DIM
SSM_CONV_CH = SSM_INNER + 2 * SSM_GROUPS * SSM_STATE
RET_HEADS = 4
RET_DK = 64
RET_DV = BRANCH_W // RET_HEADS
ROPE_BASE = 10000.0

SEG_WIDTHS = (
    GLA_HEADS * GLA_DK, GLA_HEADS * GLA_DK, BRANCH_W, GLA_RANK, BRANCH_W,
    HG_HEADS * HG_DK, HG_HEADS * HG_DK, BRANCH_W, BRANCH_W,
    BRANCH_W, SSM_CONV_CH, SSM_HEADS,
    RET_HEADS * RET_DK, RET_HEADS * RET_DK, BRANCH_W, BRANCH_W,
    N_BRANCH * D_MODEL,
)
N_IN = sum(SEG_WIDTHS)

kernel_name = 'hybrid_gla_hgrn2_ssd_retention_block'


def rms_norm(x, w):
    xf = x.astype(jnp.float32)
    y = xf * lax.rsqrt(jnp.mean(xf * xf, axis=-1, keepdims=True) + EPS)
    return (y * w.astype(jnp.float32)).astype(x.dtype)


def split_heads(t, n):
    b, s, _ = t.shape
    return t.reshape(b, s, n, -1).transpose(0, 2, 1, 3)


def merge_heads(t):
    b, h, s, d = t.shape
    return t.transpose(0, 2, 1, 3).reshape(b, s, h * d)


def split_columns(proj):
    idx = np.cumsum(SEG_WIDTHS)[:-1].tolist()
    return jnp.split(proj, idx, axis=-1)


def rotary_tables(s):
    inv = 1.0 / (ROPE_BASE ** (jnp.arange(0, RET_DK, 2, dtype=jnp.float32) / RET_DK))
    ang = jnp.arange(s, dtype=jnp.float32)[:, None] * inv[None, :]
    return jnp.cos(ang), jnp.sin(ang)


def apply_rotary(t, cos, sin):
    t1, t2 = jnp.split(t, 2, axis=-1)
    return jnp.concatenate([t1 * cos - t2 * sin, t1 * sin + t2 * cos], axis=-1)


def carry_chunk_states(local, decay):
    def step(state, inp):
        u, d = inp
        return d * state + u, state
    _, prev = lax.scan(step, jnp.zeros_like(local[0]), (local, decay))
    return prev


def gated_linear_attention(q, k, v, log_g):
    b, h, s, dk = q.shape
    dv = v.shape[-1]
    n = s // CHUNK

    def chunk_major(t):
        return t.reshape(b, h, n, CHUNK, t.shape[-1]).transpose(2, 0, 1, 3, 4)

    causal = jnp.tril(jnp.ones((CHUNK, CHUNK), dtype=bool))[:, :, None]

    def step(state, inp):
        qc, kc, vc, gc = inp
        cum = jnp.cumsum(gc.astype(jnp.float32), axis=2)
        rel = cum[:, :, :, None, :] - cum[:, :, None, :, :]
        decay = jnp.where(causal, jnp.exp(jnp.where(causal, rel, 0.0)), 0.0)
        scores = jnp.einsum('bhtk,bhsk,bhtsk->bhts', qc, kc, decay)
        o = (jnp.einsum('bhts,bhsv->bhtv', scores, vc)
             + jnp.einsum('bhtk,bhkv->bhtv', qc * jnp.exp(cum), state))
        last = cum[:, :, -1:, :]
        state = (jnp.exp(last[:, :, 0, :])[..., None] * state
                 + jnp.einsum('bhsk,bhsv->bhkv', kc * jnp.exp(last - cum), vc))
        return state, o

    init = jnp.zeros((b, h, dk, dv), jnp.float32)
    _, o = lax.scan(step, init, (chunk_major(q), chunk_major(k), chunk_major(v), chunk_major(log_g)))
    return o.transpose(1, 2, 0, 3, 4).reshape(b, h, s, dv)


def segsum_decay(a):
    t = a.shape[-1]
    strict = jnp.tril(jnp.ones((t, t), dtype=bool), -1)
    incl = jnp.tril(jnp.ones((t, t), dtype=bool))
    x = jnp.where(strict, jnp.broadcast_to(a[..., None], a.shape + (t,)), 0.0)
    ss = jnp.cumsum(x, axis=-2)
    return jnp.where(incl, jnp.exp(jnp.where(incl, ss, 0.0)), 0.0)


def causal_depthwise_conv(u, w, bias):
    out = lax.conv_general_dilated(
        u, w[:, None, :].astype(u.dtype), window_strides=(1,), padding=[(SSM_CONV - 1, 0)],
        dimension_numbers=('NWC', 'WIO', 'NWC'), feature_group_count=u.shape[-1])
    return out + bias.astype(out.dtype)


def gla_branch(q, k, v, lr, gate, w_lr, b_lr, norm_w):
    log_g = jax.nn.log_sigmoid((lr @ w_lr + b_lr).astype(jnp.float32)) / GLA_TAU
    o = gated_linear_attention(split_heads(q, GLA_HEADS) * GLA_DK ** -0.5, split_heads(k, GLA_HEADS),
                               split_heads(v, GLA_HEADS), split_heads(log_g, GLA_HEADS))
    return merge_heads(rms_norm(o, norm_w)) * jax.nn.silu(gate)


def hgrn2_branch(q, f, i, gate, lb, norm_w):
    lb = lb.astype(jnp.float32)
    ff = f.astype(jnp.float32)
    log_f = jnp.logaddexp(jax.nn.log_sigmoid(ff),
                          jnp.log(jnp.maximum(lb, LB_FLOOR)) + jax.nn.log_sigmoid(-ff))
    k = -jnp.expm1(log_f)
    o = gated_linear_attention(split_heads(q, HG_HEADS), split_heads(k, HG_HEADS),
                               split_heads(i, HG_HEADS), split_heads(log_f, HG_HEADS))
    return merge_heads(rms_norm(o, norm_w)) * jax.nn.silu(gate)


def mamba2_branch(z, xbc, dt, conv_w, conv_b, dt_bias, a_log, d_skip, norm_w):
    b, s, _ = xbc.shape
    n = s // CHUNK
    G, E, P, N = SSM_GROUPS, SSM_HEADS // SSM_GROUPS, SSM_HEAD_DIM, SSM_STATE
    xbc = jax.nn.silu(causal_depthwise_conv(xbc, conv_w, conv_b))
    xs, bm, cm = jnp.split(xbc, [SSM_INNER, SSM_INNER + G * N], axis=-1)
    dt = jax.nn.softplus(dt.astype(jnp.float32) + dt_bias.astype(jnp.float32))
    a = -jnp.exp(a_log.astype(jnp.float32))
    xh = xs.reshape(b, n, CHUNK, G, E, P)
    xdt = xh * dt.reshape(b, n, CHUNK, G, E)[..., None]
    bm = bm.reshape(b, n, CHUNK, G, N)
    cm = cm.reshape(b, n, CHUNK, G, N)
    a_dt = (dt * a).reshape(b, n, CHUNK, G, E).transpose(0, 3, 4, 1, 2)
    a_cum = jnp.cumsum(a_dt, axis=-1)
    L = segsum_decay(a_dt)
    y_diag = jnp.einsum('bnlgm,bnsgm,bgenls,bnsgep->bnlgep', cm, bm, L, xdt)
    decay_states = jnp.exp(a_cum[..., -1:] - a_cum)
    local = jnp.einsum('bnsgm,bgens,bnsgep->bngepm', bm, decay_states, xdt)
    chunk_decay = jnp.exp(a_cum[..., -1])
    prev = carry_chunk_states(jnp.moveaxis(local, 1, 0),
                              jnp.moveaxis(chunk_decay, 3, 0)[..., None, None])
    y_off = jnp.einsum('bnlgm,nbgepm,bgenl->bnlgep', cm, prev, jnp.exp(a_cum))
    y = y_diag + y_off + d_skip.reshape(G, E, 1) * xh
    y = y.reshape(b, s, SSM_INNER) * jax.nn.silu(z)
    y = rms_norm(y.reshape(b, s, G, SSM_INNER // G), norm_w.reshape(G, SSM_INNER // G))
    return y.reshape(b, s, SSM_INNER)


def chunk_retention(q, k, v, log_gamma):
    b, h, s, dk = q.shape
    dv = v.shape[-1]
    n = s // CHUNK
    qc = q.reshape(b, h, n, CHUNK, dk)
    kc = k.reshape(b, h, n, CHUNK, dk)
    vc = v.reshape(b, h, n, CHUNK, dv)
    idx = jnp.arange(CHUNK, dtype=jnp.float32)
    lg = log_gamma[:, None, None, None]
    dmat = jnp.exp(lg * jnp.abs(idx[:, None] - idx[None, :]))
    scores = jnp.einsum('bhntk,bhnsk->bhnts', qc, kc) * dmat
    o_intra = jnp.einsum('bhnts,bhnsv->bhntv', scores, vc)
    kdec = jnp.exp(lg * (CHUNK - 1 - idx)[None, None, :, None])
    local = jnp.einsum('bhnsk,bhnsv->bhnkv', kc * kdec, vc)
    cdec = jnp.broadcast_to(jnp.exp(log_gamma * CHUNK)[None, None, :, None, None], (n, 1, h, 1, 1))
    prev = jnp.moveaxis(carry_chunk_states(jnp.moveaxis(local, 2, 0), cdec), 0, 2)
    qdec = jnp.exp(lg * (idx + 1.0)[None, None, :, None])
    o_inter = jnp.einsum('bhntk,bhnkv->bhntv', qc * qdec, prev)
    return (o_intra + o_inter).reshape(b, h, s, dv)


def retention_branch(q, k, v, gate, cos, sin, log_gamma, norm_w):
    qh = apply_rotary(split_heads(q, RET_HEADS), cos, sin)
    kh = apply_rotary(split_heads(k, RET_HEADS), cos, sin) * RET_DK ** -0.5
    o = chunk_retention(qh, kh, split_heads(v, RET_HEADS), log_gamma)
    return merge_heads(rms_norm(o, norm_w)) * jax.nn.silu(gate)


def setup_inputs(seed: int = 0) -> dict:
    key = jax.random.key(seed)
    ks = jax.random.split(key, 20)
    f32 = jnp.float32

    def nrm(k, shape, scale):
        return jax.random.normal(k, shape, f32) * scale

    u = jax.random.uniform(ks[10], (DEPTH, SSM_HEADS), f32)
    dt0 = jnp.exp(u * (jnp.log(0.1) - jnp.log(0.001)) + jnp.log(0.001))
    return {
        'x': nrm(ks[0], (BATCH, SEQ, D_MODEL), 1.0),
        'norm_w': 1.0 + nrm(ks[1], (DEPTH, D_MODEL), 0.02),
        'w_in': nrm(ks[2], (DEPTH, D_MODEL, N_IN), D_MODEL ** -0.5),
        'gla_w_lr': nrm(ks[3], (DEPTH, GLA_RANK, GLA_HEADS * GLA_DK), GLA_RANK ** -0.5),
        'gla_b_lr': nrm(ks[4], (DEPTH, GLA_HEADS * GLA_DK), 0.1),
        'gla_norm': 1.0 + nrm(ks[5], (DEPTH, GLA_DV), 0.02),
        'hg_lb_logits': nrm(ks[6], (DEPTH, HG_HEADS * HG_DK), 0.5),
        'hg_norm': 1.0 + nrm(ks[7], (DEPTH, HG_DV), 0.02),
        'ssm_conv_w': nrm(ks[8], (DEPTH, SSM_CONV, SSM_CONV_CH), SSM_CONV ** -0.5),
        'ssm_conv_b': nrm(ks[9], (DEPTH, SSM_CONV_CH), 0.02),
        'ssm_dt_bias': dt0 + jnp.log(-jnp.expm1(-dt0)),
        'ssm_a_log': jnp.log(jax.random.uniform(ks[11], (DEPTH, SSM_HEADS), f32, 1.0, 16.0)),
        'ssm_d': 1.0 + nrm(ks[12], (DEPTH, SSM_HEADS), 0.02),
        'ssm_norm': 1.0 + nrm(ks[13], (DEPTH, SSM_INNER), 0.02),
        'ret_norm': 1.0 + nrm(ks[14], (DEPTH, RET_DV), 0.02),
        'w_up': nrm(ks[15], (DEPTH, N_BRANCH, BRANCH_W, D_MODEL), BRANCH_W ** -0.5),
        'w_out': nrm(ks[16], (DEPTH, D_MODEL, D_MODEL), D_MODEL ** -0.5),
        'final_norm': 1.0 + nrm(ks[17], (D_MODEL,), 0.02),
    }


def reference(x, norm_w, w_in, gla_w_lr, gla_b_lr, gla_norm, hg_lb_logits, hg_norm,
              ssm_conv_w, ssm_conv_b, ssm_dt_bias, ssm_a_log, ssm_d, ssm_norm,
              ret_norm, w_up, w_out, final_norm):
    b, s, _ = x.shape
    p = jax.nn.softmax(hg_lb_logits.astype(jnp.float32), axis=0)
    lower_bounds = jnp.cumsum(p, axis=0) - p
    cos, sin = rotary_tables(s)
    log_gamma = jnp.log(1.0 - 2.0 ** (-5.0 - jnp.arange(RET_HEADS, dtype=jnp.float32)))
    for l in range(DEPTH):
        h = rms_norm(x, norm_w[l])
        proj = h @ w_in[l]
        (gq, gk, gv, glr, gg, hq, hf, hi, hg, sz, sxbc, sdt, rq, rk, rv, rg, mg) = split_columns(proj)
        o_a = gla_branch(gq, gk, gv, glr, gg, gla_w_lr[l], gla_b_lr[l], gla_norm[l])
        o_b = hgrn2_branch(hq, hf, hi, hg, lower_bounds[l], hg_norm[l])
        o_c = mamba2_branch(sz, sxbc, sdt, ssm_conv_w[l], ssm_conv_b[l], ssm_dt_bias[l],
                            ssm_a_log[l], ssm_d[l], ssm_norm[l])
        o_d = retention_branch(rq, rk, rv, rg, cos, sin, log_gamma, ret_norm[l])
        outs = jnp.stack([o_a, o_b, o_c, o_d], axis=2).astype(h.dtype)
        up = jnp.einsum('bsnw,nwd->bsnd', outs, w_up[l])
        gates = jax.nn.sigmoid(mg.reshape(b, s, N_BRANCH, D_MODEL))
        merged = jnp.sum(gates * up, axis=2)
        x = x + (merged @ w_out[l]).astype(x.dtype)
    return rms_norm(x, final_norm)
```

```python
import functools

import numpy as np
import jax
import jax.numpy as jnp
from jax import lax
from jax.experimental import pallas as pl
from jax.experimental.pallas import tpu as pltpu

F32 = jnp.float32
BF16 = jnp.bfloat16

D_MODEL = 1024
CHUNK = 64
EPS = 1e-6
LB_FLOOR = 1e-30
BRANCH_W = 512
N_HEADS = 4
DK = 64
DV = 128
QK_W = N_HEADS * DK
GLA_RANK = 16
GLA_TAU = 16.0
SSM_HEADS = 8
SSM_P = 64
SSM_GROUPS = 2
SSM_N = 64
SSM_CONV = 4
SSM_XBC = BRANCH_W + 2 * SSM_GROUPS * SSM_N
ROPE_BASE = 10000.0
LEVELS = (32, 16, 8, 4, 2, 1)

_REF_SEGS = (('gq', 256), ('gk', 256), ('gv', 512), ('glr', 16), ('gg', 512),
             ('hq', 256), ('hf', 256), ('hi', 512), ('hg', 512),
             ('sz', 512), ('sxbc', 768), ('sdt', 8),
             ('rq', 256), ('rk', 256), ('rv', 512), ('rg', 512), ('mg', 4096))
_PM_SEGS = ('gq', 'gk', 'gv', 'gg', 'hq', 'hi', 'hg', 'sz', 'sxbc', 'rq', 'rk', 'rv', 'rg')
_PF_SLOTS = (('hf', 256), ('glr', 128), ('sdt', 128))
PM_W = 5632
PF_W = 512
MG_W = 4096
N_PROJ = PM_W + PF_W + MG_W
PROJ_TN = 1024
PM_BLOCKS = (PM_W + PF_W) // PROJ_TN


def _ref_offsets():
    off, out = 0, {}
    for name, w in _REF_SEGS:
        out[name] = (off, w)
        off += w
    return out


def _pm_offsets():
    ref = _ref_offsets()
    off, out = 0, {}
    for name in _PM_SEGS:
        out[name] = off
        off += ref[name][1]
    assert off == PM_W
    return out


PM_OFF = _pm_offsets()
PF_OFF = {'hf': 0, 'glr': 256, 'sdt': 384}


def _regroup_w_in(w_in):
    ref = _ref_offsets()
    lead = w_in.shape[:-1]
    parts = [w_in[..., ref[name][0]:ref[name][0] + ref[name][1]] for name in _PM_SEGS]
    for name, slot in _PF_SLOTS:
        o, w = ref[name]
        parts.append(w_in[..., o:o + w])
        if slot > w:
            parts.append(jnp.zeros(lead + (slot - w,), w_in.dtype))
    o, w = ref['mg']
    parts.append(w_in[..., o:o + w])
    out = jnp.concatenate(parts, axis=-1).astype(BF16)
    assert out.shape[-1] == N_PROJ
    return out


def _level_matrices():
    t = np.arange(CHUNK)[:, None]
    s = np.arange(CHUNK)[None, :]
    tri = (s <= t).astype(np.float32)
    blocks = [tri, (s > t).astype(np.float32)]
    for m in LEVELS:
        r = (t // (2 * m)) * (2 * m) + m - 1
        blocks.append(tri - (s <= r).astype(np.float32))
    a = np.concatenate(blocks, axis=0)
    return np.concatenate([a, a, a], axis=1)


def _level_masks():
    t = np.arange(CHUNK)[:, None]
    s = np.tile(np.arange(CHUNK), N_HEADS)[None, :]
    out = []
    for m in LEVELS:
        second = ((t % (2 * m)) >= m).astype(np.float32) * np.ones((1, QK_W), np.float32)
        out += [second, 1.0 - second, ((t // (2 * m)) == (s // (2 * m))).astype(np.float32)]
    out.append((t == s).astype(np.float32))
    return np.concatenate(out, axis=0)


def _lane_group_masks(n_groups, width):
    return (np.arange(n_groups * width)[None, :] // width == np.arange(n_groups)[:, None]).astype(np.float32)


def _split3(x):
    hi = x.astype(BF16)
    r = x - hi.astype(F32)
    mid = r.astype(BF16)
    lo = (r - mid.astype(F32)).astype(BF16)
    return hi, mid, lo


def _dot(a, b):
    return jnp.dot(a, b, preferred_element_type=F32)


def _dot_nt(a, b):
    return lax.dot_general(a, b, (((1,), (1,)), ((), ())), preferred_element_type=F32)


def _dot_tn(a, b):
    return lax.dot_general(a, b, (((0,), (0,)), ((), ())), preferred_element_type=F32)


def _sigmoid(x):
    return 1.0 / (1.0 + jnp.exp(-x))


def _silu(x):
    return x * _sigmoid(x)


def _log_sigmoid(x):
    return jnp.minimum(x, 0.0) - jnp.log1p(jnp.exp(-jnp.abs(x)))


def _softplus(x):
    return jnp.maximum(x, 0.0) + jnp.log1p(jnp.exp(-jnp.abs(x)))


def _group_rms(o, width):
    parts = []
    for h in range(o.shape[-1] // width):
        oh = o[:, h * width:(h + 1) * width]
        ms = jnp.mean(oh * oh, axis=-1, keepdims=True)
        parts.append(oh * lax.rsqrt(ms + EPS))
    return jnp.concatenate(parts, axis=1)


def _row_blockdiag(x_bf, masks_bf):
    return jnp.concatenate([x_bf * masks_bf[g:g + 1, :] for g in range(masks_bf.shape[0])], axis=0)


def _inproj_kernel(x_ref, nw_ref, w_ref, pm_ref, pf_ref, mg_ref, h_ref):
    j = pl.program_id(1)

    @pl.when(j == 0)
    def _():
        x = x_ref[...]
        ms = jnp.mean(x * x, axis=-1, keepdims=True)
        h_ref[...] = ((x * lax.rsqrt(ms + EPS)) * nw_ref[...]).astype(BF16)

    res = _dot(h_ref[...], w_ref[...])

    @pl.when(j < PM_BLOCKS)
    def _():
        pm_ref[...] = res.astype(BF16)

    @pl.when(j == PM_BLOCKS - 1)
    def _():
        pf_ref[...] = res[:, PROJ_TN - PF_W:]

    @pl.when(j >= PM_BLOCKS)
    def _():
        mg_ref[...] = res.astype(BF16)


def _inproj(x2, nw, w, tm):
    m = x2.shape[0]
    n_j = N_PROJ // PROJ_TN
    return pl.pallas_call(
        _inproj_kernel,
        grid=(m // tm, n_j),
        in_specs=[
            pl.BlockSpec((tm, D_MODEL), lambda i, j: (i, 0)),
            pl.BlockSpec((1, D_MODEL), lambda i, j: (0, 0)),
            pl.BlockSpec((D_MODEL, PROJ_TN), lambda i, j: (0, j)),
        ],
        out_specs=[
            pl.BlockSpec((tm, PROJ_TN), lambda i, j: (i, jnp.minimum(j, PM_BLOCKS - 1))),
            pl.BlockSpec((tm, PF_W), lambda i, j: (i, 0)),
            pl.BlockSpec((tm, PROJ_TN), lambda i, j: (i, jnp.maximum(j - PM_BLOCKS, 0))),
        ],
        out_shape=[
            jax.ShapeDtypeStruct((m, PM_W + PF_W), BF16),
            jax.ShapeDtypeStruct((m, PF_W), F32),
            jax.ShapeDtypeStruct((m, MG_W), BF16),
        ],
        scratch_shapes=[pltpu.VMEM((tm, D_MODEL), BF16)],
        compiler_params=pltpu.CompilerParams(
            dimension_semantics=("arbitrary", "arbitrary"),
            vmem_limit_bytes=48 * 1024 * 1024),
        name="inproj",
    )(x2, nw, w)


def _gla_chunk(q, k, v_bf, g, amat, lvm_ref, hm256, hm512, j256, s_ref, sbd_ref):
    hi, mid, lo = _split3(g)
    d = _dot(amat, jnp.concatenate([hi, mid, lo], axis=0))
    cum = d[0:CHUNK]
    tail = d[CHUNK:2 * CHUNK]
    o = _dot((q * jnp.exp(cum)).astype(BF16), sbd_ref[...])
    scores = _dot((q * k).astype(BF16), j256) * lvm_ref[pl.ds(3 * len(LEVELS) * CHUNK, CHUNK), :]
    for li in range(len(LEVELS)):
        e = jnp.exp(-jnp.abs(d[(2 + li) * CHUNK:(3 + li) * CHUNK]))
        base = 3 * li * CHUNK
        qs = (q * e * lvm_ref[pl.ds(base, CHUNK), :]).astype(BF16)
        ks = (k * e * lvm_ref[pl.ds(base + CHUNK, CHUNK), :]).astype(BF16)
        sc = _dot_nt(qs, _row_blockdiag(ks, hm256))
        scores = scores + sc * lvm_ref[pl.ds(base + 2 * CHUNK, CHUNK), :]
    o = o + _dot(scores.astype(BF16), _row_blockdiag(v_bf, hm512))
    upd = _dot_tn((k * jnp.exp(tail)).astype(BF16), v_bf)
    dec = jnp.exp(cum[CHUNK - 1:CHUNK, :])
    dcol = jnp.broadcast_to(dec, (DV, QK_W)).T
    for h in range(N_HEADS):
        r = slice(h * DK, (h + 1) * DK)
        c = slice(h * DV, (h + 1) * DV)
        new = s_ref[r, :] * dcol[r, :] + upd[r, c]
        s_ref[r, :] = new
        sbd_ref[r, c] = new.astype(BF16)
    return o


def _ret_chunk(q, k, v_bf, dm, qdec, kdec, cdec, hm256, hm512, s_ref, sbd_ref):
    o = _dot((q * qdec).astype(BF16), sbd_ref[...])
    scores = _dot_nt(q.astype(BF16), _row_blockdiag(k.astype(BF16), hm256)) * dm
    o = o + _dot(scores.astype(BF16), _row_blockdiag(v_bf, hm512))
    upd = _dot_tn((k * kdec).astype(BF16), v_bf)
    for h in range(N_HEADS):
        r = slice(h * DK, (h + 1) * DK)
        c = slice(h * DV, (h + 1) * DV)
        new = s_ref[r, :] * cdec[h] + upd[r, c]
        s_ref[r, :] = new
        sbd_ref[r, c] = new.astype(BF16)
    return o


def _ssd_chunk(xs, bm, cm, dtx, a_row, d_row, tri3, ones3, trit, caus, gm128, hm8, gmask, ss_ref, ssbd_ref):
    adt = dtx * a_row
    hi, mid, lo = _split3(adt)
    acol = _dot(tri3, jnp.concatenate([hi, mid, lo], axis=0))
    hi, mid, lo = _split3(adt * trit)
    arow = _dot(ones3, jnp.concatenate([hi, mid, lo], axis=0))
    decay = jnp.exp(jnp.minimum(acol - arow, 0.0)) * caus
    cm_bf = cm.astype(BF16)
    bbd = jnp.concatenate([bm.astype(BF16) * gm128[h // (SSM_HEADS // SSM_GROUPS):h // (SSM_HEADS // SSM_GROUPS) + 1, :]
                           for h in range(SSM_HEADS)], axis=0)
    cb = _dot_nt(cm_bf, bbd)
    xdt = xs * dtx
    y = _dot((cb * decay).astype(BF16), _row_blockdiag(xdt.astype(BF16), hm8))
    y = y + jnp.exp(acol) * _dot(cm_bf, ssbd_ref[...])
    y = y + d_row * xs
    alast = acol[CHUNK - 1:CHUNK, :]
    upd = _dot_tn(bm.astype(BF16), (xdt * jnp.exp(alast - acol)).astype(BF16))
    new = ss_ref[...] * jnp.exp(alast) + upd * gmask
    ss_ref[...] = new
    ssbd_ref[...] = new.astype(BF16)
    return y


def _mixer_kernel(pm_ref, pf_ref, cos_ref, sin_ref,
                  wlr_ref, blr_ref, gnorm_ref, lbl_ref, hnorm_ref,
                  convw_ref, convb_ref, dtb_ref, alog_ref, dsk_ref, snorm_ref, rnorm_ref,
                  amat_ref, lvm_ref, hm256_ref, hm512_ref, hm8_ref, gm128_ref, j256_ref,
                  dm_ref, qdec_ref, kdec_ref,
                  tri3_ref, ones3_ref, trit_ref, caus_ref, gmask_ref, dtexp_ref, lanelo_ref,
                  out_ref,
                  sa_ref, sabd_ref, sb_ref, sbbd_ref, sd_ref, sdbd_ref, ss_ref, ssbd_ref,
                  xpad_ref, conv_ref, *, layer, tile, cdec):
    i = pl.program_id(1)

    @pl.when(i == 0)
    def _():
        for r in (sa_ref, sabd_ref, sb_ref, sbbd_ref, sd_ref, sdbd_ref, ss_ref, ssbd_ref):
            r[...] = jnp.zeros(r.shape, r.dtype)
        xpad_ref[0:8, :] = jnp.zeros((8, SSM_XBC), F32)

    xo = PM_OFF['sxbc']
    xpad_ref[8:8 + tile, :] = pm_ref[:, xo:xo + SSM_XBC].astype(F32)
    acc = jnp.broadcast_to(convb_ref[...], (tile, SSM_XBC))
    for j in range(SSM_CONV):
        acc = acc + xpad_ref[8 - (SSM_CONV - 1) + j:8 - (SSM_CONV - 1) + j + tile, :] * convw_ref[j:j + 1, :]
    conv_ref[...] = _silu(acc)
    xpad_ref[0:8, :] = xpad_ref[tile:tile + 8, :]

    lbl = lbl_ref[...]
    ex = jnp.exp(lbl - jnp.max(lbl, axis=0, keepdims=True))
    lb = jnp.zeros((1, QK_W), F32)
    for l in range(layer):
        lb = lb + ex[l:l + 1, :]
    lb = lb / jnp.sum(ex, axis=0, keepdims=True)
    lb_floor = jnp.maximum(lb, LB_FLOOR)
    log_lb = jnp.log(lb_floor)

    amat = amat_ref[...]
    hm256 = hm256_ref[...]
    hm512 = hm512_ref[...]
    j256 = j256_ref[...]
    a_row = -jnp.exp(alog_ref[...])
    lanelo = lanelo_ref[...]

    def chunk_body(c, carry):
        r0 = pl.multiple_of(c * CHUNK, CHUNK)
        rows = pl.ds(r0, CHUNK)

        def seg(name, w):
            return pm_ref[rows, PM_OFF[name]:PM_OFF[name] + w]

        lr = pf_ref[rows, PF_OFF['glr']:PF_OFF['glr'] + 128].astype(BF16)
        g_a = _log_sigmoid(_dot(lr, wlr_ref[...]) + blr_ref[...]) * (1.0 / GLA_TAU)
        o_a = _gla_chunk(seg('gq', QK_W).astype(F32) * (DK ** -0.5), seg('gk', QK_W).astype(F32),
                         seg('gv', BRANCH_W), g_a, amat, lvm_ref, hm256, hm512, j256, sa_ref, sabd_ref)
        o_a = _group_rms(o_a, DV) * gnorm_ref[...] * _silu(seg('gg', BRANCH_W).astype(F32))
        out_ref[rows, 0:BRANCH_W] = o_a.astype(BF16)

        z = pf_ref[rows, PF_OFF['hf']:PF_OFF['hf'] + QK_W]
        la = _log_sigmoid(z)
        lsn = _log_sigmoid(-z)
        lbb = log_lb + lsn
        log_f = jnp.maximum(la, lbb) + jnp.log1p(jnp.exp(-jnp.abs(la - lbb)))
        k_b = (1.0 - lb_floor) * jnp.exp(lsn)
        o_b = _gla_chunk(seg('hq', QK_W).astype(F32), k_b, seg('hi', BRANCH_W), log_f,
                         amat, lvm_ref, hm256, hm512, j256, sb_ref, sbbd_ref)
        o_b = _group_rms(o_b, DV) * hnorm_ref[...] * _silu(seg('hg', BRANCH_W).astype(F32))
        out_ref[rows, BRANCH_W:2 * BRANCH_W] = o_b.astype(BF16)

        xbc = conv_ref[rows, :]
        dth, dtm, dtl = _split3(pf_ref[rows, PF_OFF['sdt']:PF_OFF['sdt'] + 128])
        dtx = _softplus(_dot(jnp.concatenate([dth, dtm, dtl], axis=1), dtexp_ref[...]) + dtb_ref[...])
        y = _ssd_chunk(xbc[:, 0:BRANCH_W], xbc[:, BRANCH_W:BRANCH_W + 128], xbc[:, BRANCH_W + 128:],
                       dtx, a_row, dsk_ref[...], tri3_ref[...], ones3_ref[...], trit_ref[...], caus_ref[...],
                       gm128_ref[...], hm8_ref[...], gmask_ref[...], ss_ref, ssbd_ref)
        y = y * _silu(seg('sz', BRANCH_W).astype(F32))
        y = _group_rms(y, BRANCH_W // SSM_GROUPS) * snorm_ref[...]
        out_ref[rows, 2 * BRANCH_W:3 * BRANCH_W] = y.astype(BF16)

        cos = cos_ref[rows, :]
        sin = sin_ref[rows, :]

        def rot(t):
            swapped = jnp.where(lanelo > 0.5, pltpu.roll(t, QK_W - DK // 2, 1), pltpu.roll(t, DK // 2, 1))
            return t * cos + swapped * sin

        q_d = rot(seg('rq', QK_W).astype(F32))
        k_d = rot(seg('rk', QK_W).astype(F32)) * (DK ** -0.5)
        o_d = _ret_chunk(q_d, k_d, seg('rv', BRANCH_W), dm_ref[...], qdec_ref[...], kdec_ref[...], cdec,
                         hm256, hm512, sd_ref, sdbd_ref)
        o_d = _group_rms(o_d, DV) * rnorm_ref[...] * _silu(seg('rg', BRANCH_W).astype(F32))
        out_ref[rows, 3 * BRANCH_W:4 * BRANCH_W] = o_d.astype(BF16)
        return carry

    lax.fori_loop(0, tile // CHUNK, chunk_body, 0)


def _const_spec(a):
    nd = a.ndim
    return pl.BlockSpec(a.shape, lambda b, i: (0,) * nd)


def _mixer(pm, pf, cos_t, sin_t, params, consts, *, batch, seq, layer, tile, cdec):
    n_t = seq // tile
    row_map = lambda b, i: (b * n_t + i, 0)
    small = list(params) + list(consts)
    in_specs = [
        pl.BlockSpec((tile, PM_W), row_map),
        pl.BlockSpec((tile, PF_W), row_map),
        pl.BlockSpec((tile, QK_W), lambda b, i: (i, 0)),
        pl.BlockSpec((tile, QK_W), lambda b, i: (i, 0)),
    ] + [_const_spec(a) for a in small]
    return pl.pallas_call(
        functools.partial(_mixer_kernel, layer=layer, tile=tile, cdec=cdec),
        grid=(batch, n_t),
        in_specs=in_specs,
        out_specs=pl.BlockSpec((tile, 4 * BRANCH_W), row_map),
        out_shape=jax.ShapeDtypeStruct((batch * seq, 4 * BRANCH_W), BF16),
        scratch_shapes=[
            pltpu.VMEM((QK_W, DV), F32), pltpu.VMEM((QK_W, BRANCH_W), BF16),
            pltpu.VMEM((QK_W, DV), F32), pltpu.VMEM((QK_W, BRANCH_W), BF16),
            pltpu.VMEM((QK_W, DV), F32), pltpu.VMEM((QK_W, BRANCH_W), BF16),
            pltpu.VMEM((SSM_GROUPS * SSM_N, BRANCH_W), F32), pltpu.VMEM((SSM_GROUPS * SSM_N, BRANCH_W), BF16),
            pltpu.VMEM((tile + 8, SSM_XBC), F32), pltpu.VMEM((tile, SSM_XBC), F32),
        ],
        compiler_params=pltpu.CompilerParams(
            dimension_semantics=("arbitrary", "arbitrary"),
            vmem_limit_bytes=48 * 1024 * 1024),
        name="mixer",
    )(pm, pf, cos_t, sin_t, *small)


def _merge_kernel(o_ref, mg_ref, x_ref, wup_ref, wout_ref, fn_ref, y_ref, *, final):
    merged = None
    for n in range(4):
        up = _dot(o_ref[:, n * BRANCH_W:(n + 1) * BRANCH_W], wup_ref[n])
        term = _sigmoid(mg_ref[:, n * D_MODEL:(n + 1) * D_MODEL].astype(F32)) * up
        merged = term if merged is None else merged + term
    y = x_ref[...] + _dot(merged.astype(BF16), wout_ref[...])
    if final:
        ms = jnp.mean(y * y, axis=-1, keepdims=True)
        y = (y * lax.rsqrt(ms + EPS)) * fn_ref[...]
    y_ref[...] = y


def _merge(outs, mg, x2, wup, wout, fn, *, tm, final):
    m = x2.shape[0]
    return pl.pallas_call(
        functools.partial(_merge_kernel, final=final),
        grid=(m // tm,),
        in_specs=[
            pl.BlockSpec((tm, 4 * BRANCH_W), lambda i: (i, 0)),
            pl.BlockSpec((tm, MG_W), lambda i: (i, 0)),
            pl.BlockSpec((tm, D_MODEL), lambda i: (i, 0)),
            pl.BlockSpec((4, BRANCH_W, D_MODEL), lambda i: (0, 0, 0)),
            pl.BlockSpec((D_MODEL, D_MODEL), lambda i: (0, 0)),
            pl.BlockSpec((1, D_MODEL), lambda i: (0, 0)),
        ],
        out_specs=pl.BlockSpec((tm, D_MODEL), lambda i: (i, 0)),
        out_shape=jax.ShapeDtypeStruct((m, D_MODEL), F32),
        compiler_params=pltpu.CompilerParams(
            dimension_semantics=("arbitrary",),
            vmem_limit_bytes=48 * 1024 * 1024),
        name="merge",
    )(outs, mg, x2, wup, wout, fn)


def _tile_rows(n, pref):
    t = min(pref, n)
    assert n % t == 0
    return t


def kernel(x, norm_w, w_in, gla_w_lr, gla_b_lr, gla_norm, hg_lb_logits, hg_norm, ssm_conv_w, ssm_conv_b,
           ssm_dt_bias, ssm_a_log, ssm_d, ssm_norm, ret_norm, w_up, w_out, final_norm):
    batch, seq, d = x.shape
    depth = norm_w.shape[0]
    assert d == D_MODEL and seq % CHUNK == 0
    m = batch * seq
    tile = _tile_rows(seq, 256)
    tm_proj = _tile_rows(m, 1024)
    tm_merge = _tile_rows(m, 512)

    w_in_r = _regroup_w_in(w_in)
    w_up_b = w_up.astype(BF16)
    w_out_b = w_out.astype(BF16)
    wlr = jnp.zeros((depth, 128, QK_W), F32).at[:, :GLA_RANK, :].set(gla_w_lr).astype(BF16)

    inv = 1.0 / (ROPE_BASE ** (jnp.arange(0, DK, 2, dtype=F32) / DK))
    ang = jnp.arange(seq, dtype=F32)[:, None] * inv[None, :]
    cos_t = jnp.tile(jnp.cos(ang), (1, 2 * N_HEADS))
    sin_t = jnp.tile(jnp.concatenate([-jnp.sin(ang), jnp.sin(ang)], axis=1), (1, N_HEADS))

    log_gamma = jnp.log(1.0 - 2.0 ** (-5.0 - jnp.arange(N_HEADS, dtype=F32)))
    lg_k = jnp.repeat(log_gamma, DK)[None, :]
    tpos = jnp.arange(CHUNK, dtype=F32)[:, None]
    spos = jnp.tile(jnp.arange(CHUNK, dtype=F32), N_HEADS)[None, :]
    dm = jnp.exp(lg_k * jnp.abs(tpos - spos))
    qdec = jnp.exp(lg_k * (tpos + 1.0))
    kdec = jnp.exp(lg_k * (CHUNK - 1.0 - tpos))
    cdec_np = np.exp(np.log(1.0 - 2.0 ** (-5.0 - np.arange(N_HEADS))) * CHUNK)
    cdec = tuple(float(c) for c in cdec_np)

    tri = np.tril(np.ones((CHUNK, CHUNK), np.float32))
    s_of_lane = np.tile(np.arange(CHUNK), SSM_HEADS)[None, :]
    srow = np.arange(CHUNK)[:, None]
    dtexp = np.zeros((128, BRANCH_W), np.float32)
    dtexp[np.arange(BRANCH_W) // SSM_P, np.arange(BRANCH_W)] = 1.0
    heads_per_group = SSM_HEADS // SSM_GROUPS
    gmask = (np.arange(SSM_GROUPS * SSM_N)[:, None] // SSM_N
             == np.arange(BRANCH_W)[None, :] // (SSM_P * heads_per_group)).astype(np.float32)
    consts = [
        jnp.asarray(_level_matrices(), BF16),
        jnp.asarray(_level_masks(), F32),
        jnp.asarray(_lane_group_masks(N_HEADS, DK), BF16),
        jnp.asarray(_lane_group_masks(N_HEADS, DV), BF16),
        jnp.asarray(_lane_group_masks(SSM_HEADS, SSM_P), BF16),
        jnp.asarray(_lane_group_masks(SSM_GROUPS, SSM_N), BF16),
        jnp.asarray(np.kron(np.eye(N_HEADS, dtype=np.float32), np.ones((DK, CHUNK), np.float32)), BF16),
        dm, qdec, kdec,
        jnp.asarray(np.concatenate([tri, tri, tri], axis=1), BF16),
        jnp.ones((CHUNK, 3 * CHUNK), BF16),
        jnp.asarray((srow <= s_of_lane).astype(np.float32)),
        jnp.asarray((srow >= s_of_lane).astype(np.float32)),
        jnp.asarray(gmask),
        jnp.asarray(np.concatenate([dtexp, dtexp, dtexp], axis=0), BF16),
        jnp.asarray((np.arange(QK_W)[None, :] % DK < DK // 2).astype(np.float32)),
    ]

    x2 = x.reshape(m, D_MODEL)
    for l in range(depth):
        pm, pf, mg = _inproj(x2, norm_w[l][None, :], w_in_r[l], tm_proj)
        params = [
            wlr[l], gla_b_lr[l][None, :], jnp.tile(gla_norm[l], N_HEADS)[None, :],
            hg_lb_logits, jnp.tile(hg_norm[l], N_HEADS)[None, :],
            ssm_conv_w[l], ssm_conv_b[l][None, :],
            jnp.repeat(ssm_dt_bias[l], SSM_P)[None, :], jnp.repeat(ssm_a_log[l], SSM_P)[None, :],
            jnp.repeat(ssm_d[l], SSM_P)[None, :], ssm_norm[l][None, :],
            jnp.tile(ret_norm[l], N_HEADS)[None, :],
        ]
        outs = _mixer(pm, pf, cos_t, sin_t, params, consts, batch=batch, seq=seq, layer=l, tile=tile, cdec=cdec)
        x2 = _merge(outs, mg, x2, w_up_b[l], w_out_b[l], final_norm[None, :], tm=tm_merge, final=(l == depth - 1))
    return x2.reshape(batch, seq, D_MODEL)
```

```python
import functools
import math

import numpy as np
import jax
import jax.numpy as jnp
from jax import lax
from jax.experimental import pallas as pl
from jax.experimental.pallas import tpu as pltpu

F32 = jnp.float32
BF16 = jnp.bfloat16

D_MODEL = 1024
CHUNK = 64
EPS = 1e-6
LB_FLOOR = 1e-30
BRANCH_W = 512
N_HEADS = 4
DK = 64
DV = 128
QK_W = N_HEADS * DK
GLA_RANK = 16
GLA_TAU = 16.0
SSM_HEADS = 8
SSM_P = 64
SSM_GROUPS = 2
SSM_N = 64
SSM_CONV = 4
SSM_XBC = BRANCH_W + 2 * SSM_GROUPS * SSM_N
ROPE_BASE = 10000.0
LEVELS = (32, 16, 8, 4, 2, 1)
N_LEVELS = len(LEVELS)
LOG2E = math.log2(math.e)
CONV_TAIL = 16

_REF_SEGS = (('gq', 256), ('gk', 256), ('gv', 512), ('glr', 16), ('gg', 512),
             ('hq', 256), ('hf', 256), ('hi', 512), ('hg', 512),
             ('sz', 512), ('sxbc', 768), ('sdt', 8),
             ('rq', 256), ('rk', 256), ('rv', 512), ('rg', 512), ('mg', 4096))
_PM_SEGS = ('gq', 'gk', 'gv', 'gg', 'hq', 'hi', 'hg', 'sz', 'sxbc', 'rq', 'rk', 'rv', 'rg')
_PF_SLOTS = (('hf', 256), ('glr', 128), ('sdt', 128))
PM_W = 5632
PF_W = 512
MG_W = 4096
N_PROJ = PM_W + PF_W + MG_W
PROJ_TN = 1024
PM_BLOCKS = (PM_W + PF_W) // PROJ_TN


def _ref_offsets():
    off, out = 0, {}
    for name, w in _REF_SEGS:
        out[name] = (off, w)
        off += w
    return out


def _pm_offsets():
    ref = _ref_offsets()
    off, out = 0, {}
    for name in _PM_SEGS:
        out[name] = off
        off += ref[name][1]
    assert off == PM_W
    return out


PM_OFF = _pm_offsets()
PF_OFF = {'hf': 0, 'glr': 256, 'sdt': 384}


def _regroup_w_in(w_in):
    ref = _ref_offsets()
    lead = w_in.shape[:-1]
    parts = [w_in[..., ref[name][0]:ref[name][0] + ref[name][1]] for name in _PM_SEGS]
    for name, slot in _PF_SLOTS:
        o, w = ref[name]
        parts.append(w_in[..., o:o + w])
        if slot > w:
            parts.append(jnp.zeros(lead + (slot - w,), w_in.dtype))
    o, w = ref['mg']
    parts.append(w_in[..., o:o + w])
    out = jnp.concatenate(parts, axis=-1).astype(BF16)
    assert out.shape[-1] == N_PROJ
    return out


def _level_matrices():
    t = np.arange(CHUNK)[:, None]
    s = np.arange(CHUNK)[None, :]
    blocks = [(s <= t), (s > t)]
    for m in LEVELS:
        r = (t // (2 * m)) * (2 * m) + m - 1
        blocks.append(np.where(t > r, (s > r) & (s <= t), (s > t) & (s <= r)))
    a = np.concatenate(blocks, axis=0).astype(np.float32)
    return np.concatenate([a, a, a], axis=1)


def _level_masks():
    t = np.arange(CHUNK)[:, None]
    s = np.tile(np.arange(CHUNK), N_HEADS)[None, :]
    out = []
    for m in LEVELS:
        out.append(((t // (2 * m)) == (s // (2 * m))) & ((t % (2 * m)) >= m) & ((s % (2 * m)) < m))
    out.append(t == s)
    return np.concatenate(out, axis=0).astype(np.float32)


def _conv_shift_matrix():
    sh = np.zeros((SSM_CONV * CHUNK, CONV_TAIL + CHUNK), np.float32)
    for j in range(SSM_CONV):
        sh[j * CHUNK + np.arange(CHUNK), CONV_TAIL - (SSM_CONV - 1) + j + np.arange(CHUNK)] = 1.0
    return sh


def _split3(x):
    hi = x.astype(BF16)
    r = x - hi.astype(F32)
    mid = r.astype(BF16)
    lo = (r - mid.astype(F32)).astype(BF16)
    return hi, mid, lo


def _dot(a, b):
    return jnp.dot(a, b, preferred_element_type=F32)


def _dot_nt(a, b):
    return lax.dot_general(a, b, (((1,), (1,)), ((), ())), preferred_element_type=F32)


def _dot_tn(a, b):
    return lax.dot_general(a, b, (((0,), (0,)), ((), ())), preferred_element_type=F32)


def _sigmoid(x):
    return 0.5 * jnp.tanh(0.5 * x) + 0.5


def _silu(x):
    hx = 0.5 * x
    return hx * jnp.tanh(hx) + hx


def _log_sigmoid(x):
    return jnp.minimum(x, 0.0) - jnp.log(1.0 + jnp.exp(-jnp.abs(x)))


def _softplus(x):
    return jnp.maximum(x, 0.0) + jnp.log(1.0 + jnp.exp(-jnp.abs(x)))


def _group_rms(o, width):
    parts = []
    for h in range(o.shape[-1] // width):
        oh = o[:, h * width:(h + 1) * width]
        ms = jnp.mean(oh * oh, axis=-1, keepdims=True)
        parts.append(oh * lax.rsqrt(ms + EPS))
    return jnp.concatenate(parts, axis=1)


def _store_blockdiag(dst_ref, x, n_blocks, lane_w, lane_of_block=None):
    for b in range(n_blocks):
        g = b if lane_of_block is None else lane_of_block(b)
        dst_ref[b * CHUNK:(b + 1) * CHUNK, g * lane_w:(g + 1) * lane_w] = x[:, g * lane_w:(g + 1) * lane_w]


def _inproj_kernel(x_ref, nw_ref, w_ref, pm_ref, pf_ref, mg_ref, h_ref):
    j = pl.program_id(1)

    @pl.when(j == 0)
    def _():
        x = x_ref[...]
        ms = jnp.mean(x * x, axis=-1, keepdims=True)
        h_ref[...] = ((x * lax.rsqrt(ms + EPS)) * nw_ref[...]).astype(BF16)

    res = _dot(h_ref[...], w_ref[...])

    @pl.when(j < PM_BLOCKS)
    def _():
        pm_ref[...] = res.astype(BF16)

    @pl.when(j == PM_BLOCKS - 1)
    def _():
        pf_ref[...] = res[:, PROJ_TN - PF_W:]

    @pl.when(j >= PM_BLOCKS)
    def _():
        mg_ref[...] = res.astype(BF16)


def _inproj(x2, nw, w, tm):
    m = x2.shape[0]
    n_j = N_PROJ // PROJ_TN
    return pl.pallas_call(
        _inproj_kernel,
        grid=(m // tm, n_j),
        in_specs=[
            pl.BlockSpec((tm, D_MODEL), lambda i, j: (i, 0)),
            pl.BlockSpec((1, D_MODEL), lambda i, j: (0, 0)),
            pl.BlockSpec((D_MODEL, PROJ_TN), lambda i, j: (0, j)),
        ],
        out_specs=[
            pl.BlockSpec((tm, PROJ_TN), lambda i, j: (i, jnp.minimum(j, PM_BLOCKS - 1))),
            pl.BlockSpec((tm, PF_W), lambda i, j: (i, 0)),
            pl.BlockSpec((tm, PROJ_TN), lambda i, j: (i, jnp.maximum(j - PM_BLOCKS, 0))),
        ],
        out_shape=[
            jax.ShapeDtypeStruct((m, PM_W + PF_W), BF16),
            jax.ShapeDtypeStruct((m, PF_W), F32),
            jax.ShapeDtypeStruct((m, MG_W), BF16),
        ],
        scratch_shapes=[pltpu.VMEM((tm, D_MODEL), BF16)],
        compiler_params=pltpu.CompilerParams(
            dimension_semantics=("arbitrary", "arbitrary"),
            vmem_limit_bytes=48 * 1024 * 1024),
        name="inproj",
    )(x2, nw, w)


def _gla_branch(prologue, finish, amat, vmask_ref, j256, s_ref, sv_ref, kbd_ref):
    q_bf, k_bf, v_bf, g2 = yield from prologue()
    hi, mid, lo = _split3(g2)
    d = _dot(amat, jnp.concatenate([hi, mid, lo], axis=0))
    yield
    e = jnp.exp2(d)
    e_bf = e.astype(BF16)
    qe = q_bf * e_bf[0:CHUNK]
    scores = _dot(q_bf * k_bf, j256)
    yield
    scores = scores * vmask_ref[N_LEVELS * CHUNK:(N_LEVELS + 1) * CHUNK, :]
    pending = []
    for li in range(N_LEVELS):
        el = e_bf[(2 + li) * CHUNK:(3 + li) * CHUNK]
        _store_blockdiag(kbd_ref.at[li], k_bf * el, N_HEADS, DK)
        pending.append((li, _dot_nt(q_bf * el, kbd_ref[li])))
        if li % 2 == 1:
            yield
            for lj, sc in pending:
                scores = scores + sc * vmask_ref[lj * CHUNK:(lj + 1) * CHUNK, :]
            pending = []
    _store_blockdiag(sv_ref.at[QK_W:2 * QK_W], v_bf, N_HEADS, DV)
    o = _dot(jnp.concatenate([qe, scores.astype(BF16)], axis=1), sv_ref[...])
    upd = _dot_tn(k_bf * e_bf[CHUNK:2 * CHUNK], v_bf)
    dcol = jnp.broadcast_to(e[CHUNK - 1:CHUNK, :], (DV, QK_W)).T
    yield
    for h in range(N_HEADS):
        r = slice(h * DK, (h + 1) * DK)
        c = slice(h * DV, (h + 1) * DV)
        new = s_ref[r, :] * dcol[r, :] + upd[r, c]
        s_ref[r, :] = new
        sv_ref[r, c] = new.astype(BF16)
    finish(o)


def _ret_branch(prologue, finish, dm_ref, qdec_ref, kdec_ref, cdec, s_ref, sv_ref, kbd_ref):
    q_bf, k_bf, v_bf = yield from prologue()
    _store_blockdiag(kbd_ref, k_bf, N_HEADS, DK)
    scores = _dot_nt(q_bf, kbd_ref[...])
    _store_blockdiag(sv_ref.at[QK_W:2 * QK_W], v_bf, N_HEADS, DV)
    yield
    o = _dot(jnp.concatenate([q_bf * qdec_ref[...], (scores * dm_ref[...]).astype(BF16)], axis=1), sv_ref[...])
    upd = _dot_tn(k_bf * kdec_ref[...], v_bf)
    yield
    for h in range(N_HEADS):
        r = slice(h * DK, (h + 1) * DK)
        c = slice(h * DV, (h + 1) * DV)
        new = s_ref[r, :] * cdec[h] + upd[r, c]
        s_ref[r, :] = new
        sv_ref[r, c] = new.astype(BF16)
    finish(o)


def _ssd_branch(prologue, finish, a2_row, d_row, tri3_ref, ones3_ref, trit_ref, caus_ref, gmask_ref,
                ss_ref, ssbd_ref, xbd_ref, bbd_ref):
    heads_per_group = SSM_HEADS // SSM_GROUPS
    xs, dtx, bm_bf, cm_bf = yield from prologue()
    adt = dtx * a2_row
    hi, mid, lo = _split3(adt)
    trit_bf = trit_ref[...]
    acol = _dot(tri3_ref[...], jnp.concatenate([hi, mid, lo], axis=0))
    arow = _dot(ones3_ref[...], jnp.concatenate([hi * trit_bf, mid * trit_bf, lo * trit_bf], axis=0))
    _store_blockdiag(bbd_ref, bm_bf, SSM_HEADS, SSM_N, lambda h: h // heads_per_group)
    cb = _dot_nt(cm_bf, bbd_ref[...])
    yoff = _dot(cm_bf, ssbd_ref[...])
    xdt = xs * dtx
    _store_blockdiag(xbd_ref, xdt.astype(BF16), SSM_HEADS, SSM_P)
    yield
    decay = jnp.exp2(jnp.minimum(acol - arow, 0.0)) * caus_ref[...]
    y = _dot((cb * decay).astype(BF16), xbd_ref[...])
    alast = acol[CHUNK - 1:CHUNK, :]
    upd = _dot_tn(bm_bf, (xdt * jnp.exp2(alast - acol)).astype(BF16))
    yield
    y = y + jnp.exp2(acol) * yoff + d_row * xs
    new = ss_ref[...] * jnp.exp2(alast) + upd * gmask_ref[...]
    ss_ref[...] = new
    ssbd_ref[...] = new.astype(BF16)
    finish(y)


def _round_robin(gens):
    live = list(gens)
    while live:
        still = []
        for g in live:
            try:
                next(g)
                still.append(g)
            except StopIteration:
                pass
        live = still


def _mixer_kernel(pm_ref, pf_ref, cos_ref, sin_ref,
                  wlr_ref, blr_ref, gnorm_ref, lbl_ref, hnorm_ref,
                  convw_ref, convb_ref, dtb_ref, alog_ref, dsk_ref, snorm_ref, rnorm_ref,
                  amat_ref, vmask_ref, j256_ref, dm_ref, qdec_ref, kdec_ref,
                  tri3_ref, ones3_ref, trit_ref, caus_ref, gmask_ref, dtexp_ref, lanelo_ref, shift_ref,
                  out_ref,
                  sa_ref, sva_ref, sb_ref, svb_ref, sd_ref, svd_ref, ss_ref, ssbd_ref,
                  kbda_ref, kbdb_ref, kbdd_ref, xbd_ref, bbd_ref, xc_ref,
                  qrot_ref, krot_ref, *, layer, tile, cdec):
    i = pl.program_id(1)

    @pl.when(i == 0)
    def _():
        for r in (sa_ref, sva_ref, sb_ref, svb_ref, sd_ref, svd_ref, ss_ref, ssbd_ref,
                  kbda_ref, kbdb_ref, kbdd_ref, xbd_ref, bbd_ref):
            r[...] = jnp.zeros(r.shape, r.dtype)
        xc_ref[0:CONV_TAIL, :] = jnp.zeros((CONV_TAIL, SSM_XBC), xc_ref.dtype)

    lbl = lbl_ref[...]
    ex = jnp.exp(lbl - jnp.max(lbl, axis=0, keepdims=True))
    lb = jnp.zeros((1, QK_W), F32)
    for l in range(layer):
        lb = lb + ex[l:l + 1, :]
    lb_floor = jnp.maximum(lb / jnp.sum(ex, axis=0, keepdims=True), LB_FLOOR)
    log_lb = jnp.log(lb_floor)

    amat = amat_ref[...]
    j256 = j256_ref[...]
    a2_row = -jnp.exp(alog_ref[...]) * LOG2E

    lanelo = lanelo_ref[...]
    for c in range(tile // CHUNK):
        rws = slice(c * CHUNK, (c + 1) * CHUNK)
        cos = cos_ref[rws, :]
        sin = sin_ref[rws, :]
        for name, dst, scale in (('rq', qrot_ref, 1.0), ('rk', krot_ref, DK ** -0.5)):
            t = pm_ref[rws, PM_OFF[name]:PM_OFF[name] + QK_W].astype(F32)
            swapped = jnp.where(lanelo > 0.5, pltpu.roll(t, QK_W - DK // 2, 1), pltpu.roll(t, DK // 2, 1))
            dst[rws, :] = ((t * cos + swapped * sin) * scale).astype(BF16)

    def chunk_body(c, carry):
        r0 = pl.multiple_of(c * CHUNK, CHUNK)
        rows = pl.ds(r0, CHUNK)

        def seg(name, w):
            return pm_ref[rows, PM_OFF[name]:PM_OFF[name] + w]

        def finish_gated(col, norm_ref, gate_name):
            def finish(o):
                o = _group_rms(o, DV) * norm_ref[...] * _silu(seg(gate_name, BRANCH_W).astype(F32))
                out_ref[rows, col * BRANCH_W:(col + 1) * BRANCH_W] = o.astype(BF16)
            return finish

        def gla_prologue():
            lr = pf_ref[rows, PF_OFF['glr']:PF_OFF['glr'] + 128].astype(BF16)
            pre = _dot(lr, wlr_ref[...])
            yield
            g2 = _log_sigmoid(pre + blr_ref[...]) * (LOG2E / GLA_TAU)
            return seg('gq', QK_W) * (DK ** -0.5), seg('gk', QK_W), seg('gv', BRANCH_W), g2

        def hgrn_prologue():
            z = pf_ref[rows, PF_OFF['hf']:PF_OFF['hf'] + QK_W]
            la = _log_sigmoid(z)
            lsn = la - z
            log_f = jnp.maximum(la, log_lb + lsn) + jnp.log(1.0 + jnp.exp(-jnp.abs(z - log_lb)))
            k_b = (1.0 - lb_floor) * jnp.exp(lsn)
            return seg('hq', QK_W), k_b.astype(BF16), seg('hi', BRANCH_W), log_f * LOG2E
            yield

        def ssd_prologue():
            xc_ref[CONV_TAIL:CONV_TAIL + CHUNK, :] = seg('sxbc', SSM_XBC)
            taps = _dot(shift_ref[...], xc_ref[...])
            xc_ref[0:CONV_TAIL, :] = xc_ref[CHUNK:CHUNK + CONV_TAIL, :]
            dth, dtm, dtl = _split3(pf_ref[rows, PF_OFF['sdt']:PF_OFF['sdt'] + 128])
            dte = _dot(jnp.concatenate([dth, dtm, dtl], axis=1), dtexp_ref[...])
            yield
            acc = jnp.broadcast_to(convb_ref[...], (CHUNK, SSM_XBC))
            for j in range(SSM_CONV):
                acc = acc + taps[j * CHUNK:(j + 1) * CHUNK] * convw_ref[j:j + 1, :]
            xbc = _silu(acc)
            dtx = _softplus(dte + dtb_ref[...])
            return (xbc[:, 0:BRANCH_W], dtx, xbc[:, BRANCH_W:BRANCH_W + 128].astype(BF16),
                    xbc[:, BRANCH_W + 128:].astype(BF16))

        def ssd_finish(y):
            y = y * _silu(seg('sz', BRANCH_W).astype(F32))
            y = _group_rms(y, BRANCH_W // SSM_GROUPS) * snorm_ref[...]
            out_ref[rows, 2 * BRANCH_W:3 * BRANCH_W] = y.astype(BF16)

        def ret_prologue():
            return qrot_ref[rows, :], krot_ref[rows, :], seg('rv', BRANCH_W)
            yield

        _round_robin([
            _gla_branch(gla_prologue, finish_gated(0, gnorm_ref, 'gg'), amat, vmask_ref, j256,
                        sa_ref, sva_ref, kbda_ref),
            _gla_branch(hgrn_prologue, finish_gated(1, hnorm_ref, 'hg'), amat, vmask_ref, j256,
                        sb_ref, svb_ref, kbdb_ref),
            _ssd_branch(ssd_prologue, ssd_finish, a2_row, dsk_ref[...], tri3_ref, ones3_ref, trit_ref, caus_ref,
                        gmask_ref, ss_ref, ssbd_ref, xbd_ref, bbd_ref),
            _ret_branch(ret_prologue, finish_gated(3, rnorm_ref, 'rg'), dm_ref, qdec_ref, kdec_ref, cdec,
                        sd_ref, svd_ref, kbdd_ref),
        ])
        return carry

    lax.fori_loop(0, tile // CHUNK, chunk_body, 0)


def _const_spec(a):
    nd = a.ndim
    return pl.BlockSpec(a.shape, lambda b, i: (0,) * nd)


def _mixer(pm, pf, cos_t, sin_t, params, consts, *, batch, seq, layer, tile, cdec):
    n_t = seq // tile
    row_map = lambda b, i: (b * n_t + i, 0)
    small = list(params) + list(consts)
    in_specs = [
        pl.BlockSpec((tile, PM_W), row_map),
        pl.BlockSpec((tile, PF_W), row_map),
        pl.BlockSpec((tile, QK_W), lambda b, i: (i, 0)),
        pl.BlockSpec((tile, QK_W), lambda b, i: (i, 0)),
    ] + [_const_spec(a) for a in small]
    state = lambda: [pltpu.VMEM((QK_W, DV), F32), pltpu.VMEM((2 * QK_W, BRANCH_W), BF16)]
    return pl.pallas_call(
        functools.partial(_mixer_kernel, layer=layer, tile=tile, cdec=cdec),
        grid=(batch, n_t),
        in_specs=in_specs,
        out_specs=pl.BlockSpec((tile, 4 * BRANCH_W), row_map),
        out_shape=jax.ShapeDtypeStruct((batch * seq, 4 * BRANCH_W), BF16),
        scratch_shapes=state() + state() + state() + [
            pltpu.VMEM((SSM_GROUPS * SSM_N, BRANCH_W), F32), pltpu.VMEM((SSM_GROUPS * SSM_N, BRANCH_W), BF16),
            pltpu.VMEM((N_LEVELS, QK_W, QK_W), BF16), pltpu.VMEM((N_LEVELS, QK_W, QK_W), BF16),
            pltpu.VMEM((QK_W, QK_W), BF16),
            pltpu.VMEM((SSM_HEADS * CHUNK, BRANCH_W), BF16), pltpu.VMEM((SSM_HEADS * CHUNK, SSM_GROUPS * SSM_N), BF16),
            pltpu.VMEM((CONV_TAIL + CHUNK, SSM_XBC), BF16),
            pltpu.VMEM((tile, QK_W), BF16), pltpu.VMEM((tile, QK_W), BF16),
        ],
        compiler_params=pltpu.CompilerParams(
            dimension_semantics=("arbitrary", "arbitrary"),
            vmem_limit_bytes=48 * 1024 * 1024),
        name="mixer",
    )(pm, pf, cos_t, sin_t, *small)


def _merge_kernel(o_ref, mg_ref, x_ref, wup_ref, wout_ref, fn_ref, y_ref, *, final):
    merged = None
    for n in range(4):
        up = _dot(o_ref[:, n * BRANCH_W:(n + 1) * BRANCH_W], wup_ref[n])
        term = _sigmoid(mg_ref[:, n * D_MODEL:(n + 1) * D_MODEL].astype(F32)) * up
        merged = term if merged is None else merged + term
    y = x_ref[...] + _dot(merged.astype(BF16), wout_ref[...])
    if final:
        ms = jnp.mean(y * y, axis=-1, keepdims=True)
        y = (y * lax.rsqrt(ms + EPS)) * fn_ref[...]
    y_ref[...] = y


def _merge(outs, mg, x2, wup, wout, fn, *, tm, final):
    m = x2.shape[0]
    return pl.pallas_call(
        functools.partial(_merge_kernel, final=final),
        grid=(m // tm,),
        in_specs=[
            pl.BlockSpec((tm, 4 * BRANCH_W), lambda i: (i, 0)),
            pl.BlockSpec((tm, MG_W), lambda i: (i, 0)),
            pl.BlockSpec((tm, D_MODEL), lambda i: (i, 0)),
            pl.BlockSpec((4, BRANCH_W, D_MODEL), lambda i: (0, 0, 0)),
            pl.BlockSpec((D_MODEL, D_MODEL), lambda i: (0, 0)),
            pl.BlockSpec((1, D_MODEL), lambda i: (0, 0)),
        ],
        out_specs=pl.BlockSpec((tm, D_MODEL), lambda i: (i, 0)),
        out_shape=jax.ShapeDtypeStruct((m, D_MODEL), F32),
        compiler_params=pltpu.CompilerParams(
            dimension_semantics=("arbitrary",),
            vmem_limit_bytes=48 * 1024 * 1024),
        name="merge",
    )(outs, mg, x2, wup, wout, fn)


def _tile_rows(n, pref):
    t = min(pref, n)
    assert n % t == 0
    return t


def kernel(x, norm_w, w_in, gla_w_lr, gla_b_lr, gla_norm, hg_lb_logits, hg_norm, ssm_conv_w, ssm_conv_b,
           ssm_dt_bias, ssm_a_log, ssm_d, ssm_norm, ret_norm, w_up, w_out, final_norm):
    batch, seq, d = x.shape
    depth = norm_w.shape[0]
    assert d == D_MODEL and seq % CHUNK == 0
    m = batch * seq
    tile = _tile_rows(seq, 256)
    tm_proj = _tile_rows(m, 1024)
    tm_merge = _tile_rows(m, 512)

    w_in_r = _regroup_w_in(w_in)
    w_up_b = w_up.astype(BF16)
    w_out_b = w_out.astype(BF16)
    wlr = jnp.zeros((depth, 128, QK_W), F32).at[:, :GLA_RANK, :].set(gla_w_lr).astype(BF16)

    inv = 1.0 / (ROPE_BASE ** (jnp.arange(0, DK, 2, dtype=F32) / DK))
    ang = jnp.arange(seq, dtype=F32)[:, None] * inv[None, :]
    cos_t = jnp.tile(jnp.cos(ang), (1, 2 * N_HEADS))
    sin_t = jnp.tile(jnp.concatenate([-jnp.sin(ang), jnp.sin(ang)], axis=1), (1, N_HEADS))

    log_gamma = jnp.log(1.0 - 2.0 ** (-5.0 - jnp.arange(N_HEADS, dtype=F32)))
    lg_k = jnp.repeat(log_gamma, DK)[None, :]
    tpos = jnp.arange(CHUNK, dtype=F32)[:, None]
    spos = jnp.tile(jnp.arange(CHUNK, dtype=F32), N_HEADS)[None, :]
    dm = jnp.exp(lg_k * jnp.abs(tpos - spos))
    qdec = jnp.exp(lg_k * (tpos + 1.0)).astype(BF16)
    kdec = jnp.exp(lg_k * (CHUNK - 1.0 - tpos)).astype(BF16)
    cdec_np = np.exp(np.log(1.0 - 2.0 ** (-5.0 - np.arange(N_HEADS))) * CHUNK)
    cdec = tuple(float(c) for c in cdec_np)

    tri = np.tril(np.ones((CHUNK, CHUNK), np.float32))
    s_of_lane = np.tile(np.arange(CHUNK), SSM_HEADS)[None, :]
    srow = np.arange(CHUNK)[:, None]
    dtexp = np.zeros((128, BRANCH_W), np.float32)
    dtexp[np.arange(BRANCH_W) // SSM_P, np.arange(BRANCH_W)] = 1.0
    heads_per_group = SSM_HEADS // SSM_GROUPS
    gmask = (np.arange(SSM_GROUPS * SSM_N)[:, None] // SSM_N
             == np.arange(BRANCH_W)[None, :] // (SSM_P * heads_per_group)).astype(np.float32)
    consts = [
        jnp.asarray(_level_matrices(), BF16),
        jnp.asarray(_level_masks(), F32),
        jnp.asarray(np.kron(np.eye(N_HEADS, dtype=np.float32), np.ones((DK, CHUNK), np.float32)), BF16),
        dm, qdec, kdec,
        jnp.asarray(np.concatenate([tri, tri, tri], axis=1), BF16),
        jnp.ones((CHUNK, 3 * CHUNK), BF16),
        jnp.asarray((srow <= s_of_lane).astype(np.float32), BF16),
        jnp.asarray((srow >= s_of_lane).astype(np.float32)),
        jnp.asarray(gmask),
        jnp.asarray(np.concatenate([dtexp, dtexp, dtexp], axis=0), BF16),
        jnp.asarray((np.arange(QK_W)[None, :] % DK < DK // 2).astype(np.float32)),
        jnp.asarray(_conv_shift_matrix(), BF16),
    ]

    x2 = x.reshape(m, D_MODEL)
    for l in range(depth):
        pm, pf, mg = _inproj(x2, norm_w[l][None, :], w_in_r[l], tm_proj)
        params = [
            wlr[l], gla_b_lr[l][None, :], jnp.tile(gla_norm[l], N_HEADS)[None, :],
            hg_lb_logits, jnp.tile(hg_norm[l], N_HEADS)[None, :],
            ssm_conv_w[l], ssm_conv_b[l][None, :],
            jnp.repeat(ssm_dt_bias[l], SSM_P)[None, :], jnp.repeat(ssm_a_log[l], SSM_P)[None, :],
            jnp.repeat(ssm_d[l], SSM_P)[None, :], ssm_norm[l][None, :],
            jnp.tile(ret_norm[l], N_HEADS)[None, :],
        ]
        outs = _mixer(pm, pf, cos_t, sin_t, params, consts, batch=batch, seq=seq, layer=l, tile=tile, cdec=cdec)
        x2 = _merge(outs, mg, x2, w_up_b[l], w_out_b[l], final_norm[None, :], tm=tm_merge, final=(l == depth - 1))
    return x2.reshape(batch, seq, D_MODEL)
```

```python
import functools
import math

import numpy as np
import jax
import jax.numpy as jnp
from jax import lax
from jax.experimental import pallas as pl
from jax.experimental.pallas import tpu as pltpu

F32 = jnp.float32
BF16 = jnp.bfloat16

D_MODEL = 1024
CHUNK = 64
EPS = 1e-6
LB_FLOOR = 1e-30
BRANCH_W = 512
N_HEADS = 4
DK = 64
DV = 128
QK_W = N_HEADS * DK
GLA_RANK = 16
GLA_TAU = 16.0
SSM_HEADS = 8
SSM_P = 64
SSM_GROUPS = 2
SSM_N = 64
SSM_CONV = 4
SSM_XBC = BRANCH_W + 2 * SSM_GROUPS * SSM_N
ROPE_BASE = 10000.0
LEVELS = (32, 16, 8, 4, 2, 1)
N_LEVELS = len(LEVELS)
LOG2E = math.log2(math.e)
CONV_TAIL = 16
PIPE_CHUNKS = 1
PIPE_SKEW = 2

_REF_SEGS = (('gq', 256), ('gk', 256), ('gv', 512), ('glr', 16), ('gg', 512),
             ('hq', 256), ('hf', 256), ('hi', 512), ('hg', 512),
             ('sz', 512), ('sxbc', 768), ('sdt', 8),
             ('rq', 256), ('rk', 256), ('rv', 512), ('rg', 512), ('mg', 4096))
_PM_SEGS = ('gq', 'gk', 'gv', 'gg', 'hq', 'hi', 'hg', 'sz', 'sxbc', 'rq', 'rk', 'rv', 'rg')
_PF_SLOTS = (('hf', 256), ('glr', 128), ('sdt', 128))
PM_W = 5632
PF_W = 512
MG_W = 4096
N_PROJ = PM_W + PF_W + MG_W
PROJ_TN = 1024
PF_BLOCK = (PM_W + PF_W) // PROJ_TN - 1
MG_OFF = PM_W + PF_W
MG_HALF = MG_W // 2
VMEM_LIMIT = 48 * 1024 * 1024


def _ref_offsets():
    off, out = 0, {}
    for name, w in _REF_SEGS:
        out[name] = (off, w)
        off += w
    return out


def _pm_offsets():
    ref = _ref_offsets()
    off, out = 0, {}
    for name in _PM_SEGS:
        out[name] = off
        off += ref[name][1]
    assert off == PM_W
    return out


PM_OFF = _pm_offsets()
PF_OFF = {'hf': 0, 'glr': 256, 'sdt': 384}


def _regroup_w_in(w_in):
    ref = _ref_offsets()
    w_in = w_in.astype(BF16)
    lead = w_in.shape[:-1]
    parts = [w_in[..., ref[name][0]:ref[name][0] + ref[name][1]] for name in _PM_SEGS]
    for name, slot in _PF_SLOTS:
        o, w = ref[name]
        parts.append(w_in[..., o:o + w])
        if slot > w:
            parts.append(jnp.zeros(lead + (slot - w,), w_in.dtype))
    o, w = ref['mg']
    parts.append(w_in[..., o:o + w])
    out = jnp.concatenate(parts, axis=-1)
    assert out.shape[-1] == N_PROJ
    return out


def _level_matrices():
    t = np.arange(CHUNK)[:, None]
    s = np.arange(CHUNK)[None, :]
    blocks = [(s <= t), (s > t)]
    for m in LEVELS:
        r = (t // (2 * m)) * (2 * m) + m - 1
        blocks.append(np.where(t > r, (s > r) & (s <= t), (s > t) & (s <= r)))
    a = np.concatenate(blocks, axis=0).astype(np.float32)
    return np.concatenate([a, a, a], axis=1)


def _level_masks():
    t = np.arange(CHUNK)[:, None]
    s = np.tile(np.arange(CHUNK), N_HEADS)[None, :]
    out = []
    for m in LEVELS:
        out.append(((t // (2 * m)) == (s // (2 * m))) & ((t % (2 * m)) >= m) & ((s % (2 * m)) < m))
    out.append(t == s)
    return np.concatenate(out, axis=0).astype(np.float32)


def _conv_shift_matrix():
    sh = np.zeros((SSM_CONV * CHUNK, CONV_TAIL + CHUNK), np.float32)
    for j in range(SSM_CONV):
        sh[j * CHUNK + np.arange(CHUNK), CONV_TAIL - (SSM_CONV - 1) + j + np.arange(CHUNK)] = 1.0
    return sh


def _split3(x):
    hi = x.astype(BF16)
    r = x - hi.astype(F32)
    mid = r.astype(BF16)
    lo = (r - mid.astype(F32)).astype(BF16)
    return hi, mid, lo


def _dot(a, b):
    return jnp.dot(a, b, preferred_element_type=F32)


def _dot_nt(a, b):
    return lax.dot_general(a, b, (((1,), (1,)), ((), ())), preferred_element_type=F32)


def _dot_tn(a, b):
    return lax.dot_general(a, b, (((0,), (0,)), ((), ())), preferred_element_type=F32)


def _sigmoid(x):
    return 0.5 * jnp.tanh(0.5 * x) + 0.5


def _silu(x):
    hx = 0.5 * x
    return hx * jnp.tanh(hx) + hx


def _log_sigmoid(x):
    return jnp.minimum(x, 0.0) - jnp.log(1.0 + jnp.exp(-jnp.abs(x)))


def _softplus(x):
    return jnp.maximum(x, 0.0) + jnp.log(1.0 + jnp.exp(-jnp.abs(x)))


def _group_rms(o, width):
    parts = []
    for h in range(o.shape[-1] // width):
        oh = o[:, h * width:(h + 1) * width]
        ms = jnp.mean(oh * oh, axis=-1, keepdims=True)
        parts.append(oh * lax.rsqrt(ms + EPS))
    return jnp.concatenate(parts, axis=1)


def _store_blockdiag(dst_ref, x, n_blocks, lane_w, lane_of_block=None):
    for b in range(n_blocks):
        g = b if lane_of_block is None else lane_of_block(b)
        dst_ref[b * CHUNK:(b + 1) * CHUNK, g * lane_w:(g + 1) * lane_w] = x[:, g * lane_w:(g + 1) * lane_w]


def _rms_rows(x, w):
    ms = jnp.mean(x * x, axis=-1, keepdims=True)
    return (x * lax.rsqrt(ms + EPS)) * w


def _norm_kernel(x_ref, nw_ref, h_ref):
    h_ref[...] = _rms_rows(x_ref[...], nw_ref[...]).astype(BF16)


def _norm(x2, nw, tm):
    m = x2.shape[0]
    return pl.pallas_call(
        _norm_kernel,
        grid=(m // tm,),
        in_specs=[pl.BlockSpec((tm, D_MODEL), lambda i: (i, 0)), pl.BlockSpec((1, D_MODEL), lambda i: (0, 0))],
        out_specs=pl.BlockSpec((tm, D_MODEL), lambda i: (i, 0)),
        out_shape=jax.ShapeDtypeStruct((m, D_MODEL), BF16),
        compiler_params=pltpu.CompilerParams(dimension_semantics=("arbitrary",)),
        name="norm",
    )(x2, nw)


def _inproj_kernel(h_ref, w_ref, proj_ref, pf_ref):
    res = _dot(h_ref[...], w_ref[...])
    proj_ref[...] = res.astype(BF16)

    @pl.when(pl.program_id(1) == PF_BLOCK)
    def _():
        pf_ref[...] = res[:, PROJ_TN - PF_W:]


def _inproj(h, w, tm):
    m = h.shape[0]
    return pl.pallas_call(
        _inproj_kernel,
        grid=(m // tm, N_PROJ // PROJ_TN),
        in_specs=[
            pl.BlockSpec((tm, D_MODEL), lambda i, j: (i, 0)),
            pl.BlockSpec((D_MODEL, PROJ_TN), lambda i, j: (0, j)),
        ],
        out_specs=[
            pl.BlockSpec((tm, PROJ_TN), lambda i, j: (i, j)),
            pl.BlockSpec((tm, PF_W), lambda i, j: (i, 0)),
        ],
        out_shape=[
            jax.ShapeDtypeStruct((m, N_PROJ), BF16),
            jax.ShapeDtypeStruct((m, PF_W), F32),
        ],
        compiler_params=pltpu.CompilerParams(
            dimension_semantics=("arbitrary", "arbitrary"),
            vmem_limit_bytes=VMEM_LIMIT),
        name="inproj",
    )(h, w)


def _gla_branch(prologue, finish, amat, vmask_ref, j256, s_ref, sv_ref, kbd_ref, kbdt_ref):
    q_bf, k_bf, v_bf, g2 = yield from prologue()
    hi, mid, lo = _split3(g2)
    d = _dot(amat, jnp.concatenate([hi, mid, lo], axis=0))
    yield
    e = jnp.exp2(d)
    e_bf = e.astype(BF16)
    qe = q_bf * e_bf[0:CHUNK]
    scores = _dot(q_bf * k_bf, j256)
    yield
    scores = scores * vmask_ref[N_LEVELS * CHUNK:(N_LEVELS + 1) * CHUNK, :]
    pending = []
    for li in range(N_LEVELS):
        el = e_bf[(2 + li) * CHUNK:(3 + li) * CHUNK]
        _store_blockdiag(kbd_ref.at[li], k_bf * el, N_HEADS, DK)
        kbdt_ref[li] = kbd_ref[li].T
        pending.append((li, _dot(q_bf * el, kbdt_ref[li])))
        if li % 2 == 1:
            yield
            for lj, sc in pending:
                scores = scores + sc * vmask_ref[lj * CHUNK:(lj + 1) * CHUNK, :]
            pending = []
    _store_blockdiag(sv_ref.at[QK_W:2 * QK_W], v_bf, N_HEADS, DV)
    o = _dot(jnp.concatenate([qe, scores.astype(BF16)], axis=1), sv_ref[...])
    upd = _dot_tn(k_bf * e_bf[CHUNK:2 * CHUNK], v_bf)
    dcol = jnp.broadcast_to(e[CHUNK - 1:CHUNK, :], (DV, QK_W)).T
    yield
    for h in range(N_HEADS):
        r = slice(h * DK, (h + 1) * DK)
        c = slice(h * DV, (h + 1) * DV)
        new = s_ref[r, :] * dcol[r, :] + upd[r, c]
        s_ref[r, :] = new
        sv_ref[r, c] = new.astype(BF16)
    finish(o)


def _ret_branch(prologue, finish, dm_ref, qdec_ref, kdec_ref, cdec, s_ref, sv_ref, kbd_ref):
    q_bf, k_bf, v_bf = yield from prologue()
    _store_blockdiag(kbd_ref, k_bf, N_HEADS, DK)
    scores = _dot_nt(q_bf, kbd_ref[...])
    _store_blockdiag(sv_ref.at[QK_W:2 * QK_W], v_bf, N_HEADS, DV)
    yield
    o = _dot(jnp.concatenate([q_bf * qdec_ref[...], (scores * dm_ref[...]).astype(BF16)], axis=1), sv_ref[...])
    upd = _dot_tn(k_bf * kdec_ref[...], v_bf)
    yield
    for h in range(N_HEADS):
        r = slice(h * DK, (h + 1) * DK)
        c = slice(h * DV, (h + 1) * DV)
        new = s_ref[r, :] * cdec[h] + upd[r, c]
        s_ref[r, :] = new
        sv_ref[r, c] = new.astype(BF16)
    finish(o)


def _ssd_branch(prologue, finish, a2_row, d_row, tri3_ref, ones3_ref, trit_ref, caus_ref, gmask_ref,
                ss_ref, ssbd_ref, xbd_ref, bbd_ref):
    heads_per_group = SSM_HEADS // SSM_GROUPS
    xs, dtx, bm_bf, cm_bf = yield from prologue()
    adt = dtx * a2_row
    hi, mid, lo = _split3(adt)
    trit_bf = trit_ref[...]
    acol = _dot(tri3_ref[...], jnp.concatenate([hi, mid, lo], axis=0))
    arow = _dot(ones3_ref[...], jnp.concatenate([hi * trit_bf, mid * trit_bf, lo * trit_bf], axis=0))
    _store_blockdiag(bbd_ref, bm_bf, SSM_HEADS, SSM_N, lambda h: h // heads_per_group)
    cb = _dot_nt(cm_bf, bbd_ref[...])
    yoff = _dot(cm_bf, ssbd_ref[...])
    xdt = xs * dtx
    _store_blockdiag(xbd_ref, xdt.astype(BF16), SSM_HEADS, SSM_P)
    yield
    decay = jnp.exp2(jnp.minimum(acol - arow, 0.0)) * caus_ref[...]
    y = _dot((cb * decay).astype(BF16), xbd_ref[...])
    alast = acol[CHUNK - 1:CHUNK, :]
    upd = _dot_tn(bm_bf, (xdt * jnp.exp2(alast - acol)).astype(BF16))
    yield
    y = y + jnp.exp2(acol) * yoff + d_row * xs
    new = ss_ref[...] * jnp.exp2(alast) + upd * gmask_ref[...]
    ss_ref[...] = new
    ssbd_ref[...] = new.astype(BF16)
    finish(y)


def _round_robin(gens):
    live = list(gens)
    while live:
        still = []
        for g in live:
            try:
                next(g)
                still.append(g)
            except StopIteration:
                pass
        live = still


def _mixer_kernel(pm_ref, pf_ref, cos_ref, sin_ref,
                  wlr_ref, blr_ref, gnorm_ref, lbl_ref, hnorm_ref,
                  convw_ref, convb_ref, dtb_ref, alog_ref, dsk_ref, snorm_ref, rnorm_ref,
                  amat_ref, vmask_ref, j256_ref, dm_ref, qdec_ref, kdec_ref,
                  tri3_ref, ones3_ref, trit_ref, caus_ref, gmask_ref, dtexp_ref, lanelo_ref, shift_ref,
                  out_ref,
                  sa_ref, sva_ref, sb_ref, svb_ref, sd_ref, svd_ref, ss_ref, ssbd_ref,
                  kbda_ref, kbdb_ref, kbdd_ref, xbd_ref, bbd_ref, xc_ref,
                  qrot_ref, krot_ref, kbdta_ref, kbdtb_ref, g2a_ref, dtx_ref, *, layer, tile, cdec):
    i = pl.program_id(1)

    @pl.when(i == 0)
    def _():
        for r in (sa_ref, sva_ref, sb_ref, svb_ref, sd_ref, svd_ref, ss_ref, ssbd_ref,
                  kbda_ref, kbdb_ref, kbdd_ref, xbd_ref, bbd_ref):
            r[...] = jnp.zeros(r.shape, r.dtype)
        xc_ref[0:CONV_TAIL, :] = jnp.zeros((CONV_TAIL, SSM_XBC), xc_ref.dtype)

    lbl = lbl_ref[...]
    ex = jnp.exp(lbl - jnp.max(lbl, axis=0, keepdims=True))
    lb = jnp.zeros((1, QK_W), F32)
    for l in range(layer):
        lb = lb + ex[l:l + 1, :]
    lb_floor = jnp.maximum(lb / jnp.sum(ex, axis=0, keepdims=True), LB_FLOOR)
    log_lb = jnp.log(lb_floor)

    amat = amat_ref[...]
    j256 = j256_ref[...]
    a2_row = -jnp.exp(alog_ref[...]) * LOG2E

    lr = pf_ref[:, PF_OFF['glr']:PF_OFF['glr'] + 128].astype(BF16)
    g2a_ref[...] = _log_sigmoid(_dot(lr, wlr_ref[...]) + blr_ref[...]) * (LOG2E / GLA_TAU)
    dth, dtm, dtl = _split3(pf_ref[:, PF_OFF['sdt']:PF_OFF['sdt'] + 128])
    dtx_ref[...] = _softplus(_dot(jnp.concatenate([dth, dtm, dtl], axis=1), dtexp_ref[...]) + dtb_ref[...])

    lanelo = lanelo_ref[...]
    for c in range(tile // CHUNK):
        rws = slice(c * CHUNK, (c + 1) * CHUNK)
        cos = cos_ref[rws, :]
        sin = sin_ref[rws, :]
        for name, dst, scale in (('rq', qrot_ref, 1.0), ('rk', krot_ref, DK ** -0.5)):
            t = pm_ref[rws, PM_OFF[name]:PM_OFF[name] + QK_W].astype(F32)
            swapped = jnp.where(lanelo > 0.5, pltpu.roll(t, QK_W - DK // 2, 1), pltpu.roll(t, DK // 2, 1))
            dst[rws, :] = ((t * cos + swapped * sin) * scale).astype(BF16)

    def chunk_branches(c):
        r0 = pl.multiple_of(c * CHUNK, CHUNK)
        rows = pl.ds(r0, CHUNK)

        def seg(name, w):
            return pm_ref[rows, PM_OFF[name]:PM_OFF[name] + w]

        def finish_gated(col, norm_ref, gate_name):
            def finish(o):
                o = _group_rms(o, DV) * norm_ref[...] * _silu(seg(gate_name, BRANCH_W).astype(F32))
                out_ref[rows, col * BRANCH_W:(col + 1) * BRANCH_W] = o.astype(BF16)
            return finish

        def gla_prologue():
            return seg('gq', QK_W) * (DK ** -0.5), seg('gk', QK_W), seg('gv', BRANCH_W), g2a_ref[rows, :]
            yield

        def hgrn_prologue():
            z = pf_ref[rows, PF_OFF['hf']:PF_OFF['hf'] + QK_W]
            la = _log_sigmoid(z)
            lsn = la - z
            log_f = jnp.maximum(la, log_lb + lsn) + jnp.log(1.0 + jnp.exp(-jnp.abs(z - log_lb)))
            k_b = (1.0 - lb_floor) * jnp.exp(lsn)
            return seg('hq', QK_W), k_b.astype(BF16), seg('hi', BRANCH_W), log_f * LOG2E
            yield

        def ssd_prologue():
            xc_ref[CONV_TAIL:CONV_TAIL + CHUNK, :] = seg('sxbc', SSM_XBC)
            taps = _dot(shift_ref[...], xc_ref[...])
            xc_ref[0:CONV_TAIL, :] = xc_ref[CHUNK:CHUNK + CONV_TAIL, :]
            yield
            acc = jnp.broadcast_to(convb_ref[...], (CHUNK, SSM_XBC))
            for j in range(SSM_CONV):
                acc = acc + taps[j * CHUNK:(j + 1) * CHUNK] * convw_ref[j:j + 1, :]
            xbc = _silu(acc)
            return (xbc[:, 0:BRANCH_W], dtx_ref[rows, :], xbc[:, BRANCH_W:BRANCH_W + 128].astype(BF16),
                    xbc[:, BRANCH_W + 128:].astype(BF16))

        def ssd_finish(y):
            y = y * _silu(seg('sz', BRANCH_W).astype(F32))
            y = _group_rms(y, BRANCH_W // SSM_GROUPS) * snorm_ref[...]
            out_ref[rows, 2 * BRANCH_W:3 * BRANCH_W] = y.astype(BF16)

        def ret_prologue():
            return qrot_ref[rows, :], krot_ref[rows, :], seg('rv', BRANCH_W)
            yield

        return [
            _gla_branch(gla_prologue, finish_gated(0, gnorm_ref, 'gg'), amat, vmask_ref, j256,
                        sa_ref, sva_ref, kbda_ref, kbdta_ref),
            _gla_branch(hgrn_prologue, finish_gated(1, hnorm_ref, 'hg'), amat, vmask_ref, j256,
                        sb_ref, svb_ref, kbdb_ref, kbdtb_ref),
            _ssd_branch(ssd_prologue, ssd_finish, a2_row, dsk_ref[...], tri3_ref, ones3_ref, trit_ref, caus_ref,
                        gmask_ref, ss_ref, ssbd_ref, xbd_ref, bbd_ref),
            _ret_branch(ret_prologue, finish_gated(3, rnorm_ref, 'rg'), dm_ref, qdec_ref, kdec_ref, cdec,
                        sd_ref, svd_ref, kbdd_ref),
        ]

    def delayed(gen, n):
        for _ in range(n):
            yield
        yield from gen

    def group_body(p, carry):
        gens = []
        for u in range(PIPE_CHUNKS):
            gens += [delayed(g, PIPE_SKEW * u) for g in chunk_branches(p * PIPE_CHUNKS + u)]
        _round_robin(gens)
        return carry

    lax.fori_loop(0, tile // (CHUNK * PIPE_CHUNKS), group_body, 0)


def _const_spec(a):
    nd = a.ndim
    return pl.BlockSpec(a.shape, lambda b, i: (0,) * nd)


def _mixer(pm, pf, cos_t, sin_t, params, consts, *, batch, seq, layer, tile, cdec):
    n_t = seq // tile
    row_map = lambda b, i: (b * n_t + i, 0)
    small = list(params) + list(consts)
    in_specs = [
        pl.BlockSpec((tile, PM_W), row_map),
        pl.BlockSpec((tile, PF_W), row_map),
        pl.BlockSpec((tile, QK_W), lambda b, i: (i, 0)),
        pl.BlockSpec((tile, QK_W), lambda b, i: (i, 0)),
    ] + [_const_spec(a) for a in small]
    state = lambda: [pltpu.VMEM((QK_W, DV), F32), pltpu.VMEM((2 * QK_W, BRANCH_W), BF16)]
    return pl.pallas_call(
        functools.partial(_mixer_kernel, layer=layer, tile=tile, cdec=cdec),
        grid=(batch, n_t),
        in_specs=in_specs,
        out_specs=pl.BlockSpec((tile, 4 * BRANCH_W), row_map),
        out_shape=jax.ShapeDtypeStruct((batch * seq, 4 * BRANCH_W), BF16),
        scratch_shapes=state() + state() + state() + [
            pltpu.VMEM((SSM_GROUPS * SSM_N, BRANCH_W), F32), pltpu.VMEM((SSM_GROUPS * SSM_N, BRANCH_W), BF16),
            pltpu.VMEM((N_LEVELS, QK_W, QK_W), BF16), pltpu.VMEM((N_LEVELS, QK_W, QK_W), BF16),
            pltpu.VMEM((QK_W, QK_W), BF16),
            pltpu.VMEM((SSM_HEADS * CHUNK, BRANCH_W), BF16), pltpu.VMEM((SSM_HEADS * CHUNK, SSM_GROUPS * SSM_N), BF16),
            pltpu.VMEM((CONV_TAIL + CHUNK, SSM_XBC), BF16),
            pltpu.VMEM((tile, QK_W), BF16), pltpu.VMEM((tile, QK_W), BF16),
            pltpu.VMEM((N_LEVELS, QK_W, QK_W), BF16), pltpu.VMEM((N_LEVELS, QK_W, QK_W), BF16),
            pltpu.VMEM((tile, QK_W), F32), pltpu.VMEM((tile, BRANCH_W), F32),
        ],
        compiler_params=pltpu.CompilerParams(
            dimension_semantics=("arbitrary", "arbitrary"),
            vmem_limit_bytes=VMEM_LIMIT),
        name="mixer",
    )(pm, pf, cos_t, sin_t, *small)


def _merge_kernel(o_ref, mga_ref, mgb_ref, x_ref, wup_ref, wout_ref, nw_ref, *out_refs, final):
    merged = None
    for n in range(4):
        up = _dot(o_ref[:, n * BRANCH_W:(n + 1) * BRANCH_W], wup_ref[n])
        mg_ref = mga_ref if n < 2 else mgb_ref
        gate = _sigmoid(mg_ref[:, (n % 2) * D_MODEL:(n % 2 + 1) * D_MODEL].astype(F32))
        merged = gate * up if merged is None else merged + gate * up
    y = x_ref[...] + _dot(merged.astype(BF16), wout_ref[...])
    yn = _rms_rows(y, nw_ref[...])
    if final:
        out_refs[0][...] = yn
    else:
        out_refs[0][...] = y
        out_refs[1][...] = yn.astype(BF16)


def _merge(outs, proj, x2, wup, wout, nw, *, tm, final):
    m = x2.shape[0]
    row = lambda i: (i, 0)
    if final:
        out_specs = [pl.BlockSpec((tm, D_MODEL), row)]
        out_shape = [jax.ShapeDtypeStruct((m, D_MODEL), F32)]
    else:
        out_specs = [pl.BlockSpec((tm, D_MODEL), row), pl.BlockSpec((tm, D_MODEL), row)]
        out_shape = [jax.ShapeDtypeStruct((m, D_MODEL), F32), jax.ShapeDtypeStruct((m, D_MODEL), BF16)]
    return pl.pallas_call(
        functools.partial(_merge_kernel, final=final),
        grid=(m // tm,),
        in_specs=[
            pl.BlockSpec((tm, 4 * BRANCH_W), row),
            pl.BlockSpec((tm, MG_HALF), lambda i: (i, MG_OFF // MG_HALF)),
            pl.BlockSpec((tm, MG_HALF), lambda i: (i, MG_OFF // MG_HALF + 1)),
            pl.BlockSpec((tm, D_MODEL), row),
            pl.BlockSpec((4, BRANCH_W, D_MODEL), lambda i: (0, 0, 0)),
            pl.BlockSpec((D_MODEL, D_MODEL), lambda i: (0, 0)),
            pl.BlockSpec((1, D_MODEL), lambda i: (0, 0)),
        ],
        out_specs=out_specs,
        out_shape=out_shape,
        compiler_params=pltpu.CompilerParams(
            dimension_semantics=("arbitrary",),
            vmem_limit_bytes=VMEM_LIMIT),
        name="merge",
    )(outs, proj, proj, x2, wup, wout, nw)


def _tile_rows(n, pref):
    t = min(pref, n)
    assert n % t == 0
    return t


def kernel(x, norm_w, w_in, gla_w_lr, gla_b_lr, gla_norm, hg_lb_logits, hg_norm, ssm_conv_w, ssm_conv_b,
           ssm_dt_bias, ssm_a_log, ssm_d, ssm_norm, ret_norm, w_up, w_out, final_norm):
    batch, seq, d = x.shape
    depth = norm_w.shape[0]
    assert d == D_MODEL and seq % CHUNK == 0
    m = batch * seq
    tile = _tile_rows(seq, 512)
    assert tile % (CHUNK * PIPE_CHUNKS) == 0 and PIPE_SKEW >= 1
    tm_proj = _tile_rows(m, 1024)
    tm_merge = _tile_rows(m, 512)

    w_in_r = _regroup_w_in(w_in)
    w_up_b = w_up.astype(BF16)
    w_out_b = w_out.astype(BF16)
    wlr = jnp.zeros((depth, 128, QK_W), F32).at[:, :GLA_RANK, :].set(gla_w_lr).astype(BF16)

    inv = 1.0 / (ROPE_BASE ** (jnp.arange(0, DK, 2, dtype=F32) / DK))
    ang = jnp.arange(seq, dtype=F32)[:, None] * inv[None, :]
    cos_t = jnp.tile(jnp.cos(ang), (1, 2 * N_HEADS))
    sin_t = jnp.tile(jnp.concatenate([-jnp.sin(ang), jnp.sin(ang)], axis=1), (1, N_HEADS))

    log_gamma = jnp.log(1.0 - 2.0 ** (-5.0 - jnp.arange(N_HEADS, dtype=F32)))
    lg_k = jnp.repeat(log_gamma, DK)[None, :]
    tpos = jnp.arange(CHUNK, dtype=F32)[:, None]
    spos = jnp.tile(jnp.arange(CHUNK, dtype=F32), N_HEADS)[None, :]
    dm = jnp.exp(lg_k * jnp.abs(tpos - spos))
    qdec = jnp.exp(lg_k * (tpos + 1.0)).astype(BF16)
    kdec = jnp.exp(lg_k * (CHUNK - 1.0 - tpos)).astype(BF16)
    cdec_np = np.exp(np.log(1.0 - 2.0 ** (-5.0 - np.arange(N_HEADS))) * CHUNK)
    cdec = tuple(float(c) for c in cdec_np)

    tri = np.tril(np.ones((CHUNK, CHUNK), np.float32))
    s_of_lane = np.tile(np.arange(CHUNK), SSM_HEADS)[None, :]
    srow = np.arange(CHUNK)[:, None]
    dtexp = np.zeros((128, BRANCH_W), np.float32)
    dtexp[np.arange(BRANCH_W) // SSM_P, np.arange(BRANCH_W)] = 1.0
    heads_per_group = SSM_HEADS // SSM_GROUPS
    gmask = (np.arange(SSM_GROUPS * SSM_N)[:, None] // SSM_N
             == np.arange(BRANCH_W)[None, :] // (SSM_P * heads_per_group)).astype(np.float32)
    consts = [
        jnp.asarray(_level_matrices(), BF16),
        jnp.asarray(_level_masks(), F32),
        jnp.asarray(np.kron(np.eye(N_HEADS, dtype=np.float32), np.ones((DK, CHUNK), np.float32)), BF16),
        dm, qdec, kdec,
        jnp.asarray(np.concatenate([tri, tri, tri], axis=1), BF16),
        jnp.ones((CHUNK, 3 * CHUNK), BF16),
        jnp.asarray((srow <= s_of_lane).astype(np.float32), BF16),
        jnp.asarray((srow >= s_of_lane).astype(np.float32)),
        jnp.asarray(gmask),
        jnp.asarray(np.concatenate([dtexp, dtexp, dtexp], axis=0), BF16),
        jnp.asarray((np.arange(QK_W)[None, :] % DK < DK // 2).astype(np.float32)),
        jnp.asarray(_conv_shift_matrix(), BF16),
    ]

    x2 = x.reshape(m, D_MODEL)
    h = _norm(x2, norm_w[0][None, :], tm_proj)
    for l in range(depth):
        proj, pf = _inproj(h, w_in_r[l], tm_proj)
        params = [
            wlr[l], gla_b_lr[l][None, :], jnp.tile(gla_norm[l], N_HEADS)[None, :],
            hg_lb_logits, jnp.tile(hg_norm[l], N_HEADS)[None, :],
            ssm_conv_w[l], ssm_conv_b[l][None, :],
            jnp.repeat(ssm_dt_bias[l], SSM_P)[None, :], jnp.repeat(ssm_a_log[l], SSM_P)[None, :],
            jnp.repeat(ssm_d[l], SSM_P)[None, :], ssm_norm[l][None, :],
            jnp.tile(ret_norm[l], N_HEADS)[None, :],
        ]
        outs = _mixer(proj, pf, cos_t, sin_t, params, consts, batch=batch, seq=seq, layer=l, tile=tile, cdec=cdec)
        if l == depth - 1:
            (x2,) = _merge(outs, proj, x2, w_up_b[l], w_out_b[l], final_norm[None, :], tm=tm_merge, final=True)
        else:
            x2, h = _merge(outs, proj, x2, w_up_b[l], w_out_b[l], norm_w[l + 1][None, :], tm=tm_merge, final=False)
    return x2.reshape(batch, seq, D_MODEL)
```

```python
import functools
import math

import numpy as np
import jax
import jax.numpy as jnp
from jax import lax
from jax.experimental import pallas as pl
from jax.experimental.pallas import tpu as pltpu

F32 = jnp.float32
BF16 = jnp.bfloat16

D_MODEL = 1024
CHUNK = 64
EPS = 1e-6
LB_FLOOR = 1e-30
BRANCH_W = 512
N_HEADS = 4
DK = 64
DV = 128
QK_W = N_HEADS * DK
GLA_RANK = 16
GLA_TAU = 16.0
SSM_HEADS = 8
SSM_P = 64
SSM_GROUPS = 2
SSM_N = 64
SSM_CONV = 4
SSM_XBC = BRANCH_W + 2 * SSM_GROUPS * SSM_N
ROPE_BASE = 10000.0
LEVELS = (32, 16, 8, 4, 2, 1)
N_LEVELS = len(LEVELS)
LOG2E = math.log2(math.e)
CONV_TAIL = 16
PIPE_CHUNKS = 1
PIPE_SKEW = 2
SAFE_LOG2 = 100.0

_REF_SEGS = (('gq', 256), ('gk', 256), ('gv', 512), ('glr', 16), ('gg', 512),
             ('hq', 256), ('hf', 256), ('hi', 512), ('hg', 512),
             ('sz', 512), ('sxbc', 768), ('sdt', 8),
             ('rq', 256), ('rk', 256), ('rv', 512), ('rg', 512), ('mg', 4096))
_PM_SEGS = ('gq', 'gk', 'gv', 'gg', 'hq', 'hi', 'hg', 'sz', 'sxbc', 'rq', 'rk', 'rv', 'rg')
_PF_SLOTS = (('hf', 256), ('glr', 128), ('sdt', 128))
PM_W = 5632
PF_W = 512
MG_W = 4096
N_PROJ = PM_W + PF_W + MG_W
PROJ_TN = 1024
PF_BLOCK = (PM_W + PF_W) // PROJ_TN - 1
MG_OFF = PM_W + PF_W
MG_HALF = MG_W // 2
VMEM_LIMIT = 48 * 1024 * 1024


def _ref_offsets():
    off, out = 0, {}
    for name, w in _REF_SEGS:
        out[name] = (off, w)
        off += w
    return out


def _pm_offsets():
    ref = _ref_offsets()
    off, out = 0, {}
    for name in _PM_SEGS:
        out[name] = off
        off += ref[name][1]
    assert off == PM_W
    return out


PM_OFF = _pm_offsets()
PF_OFF = {'hf': 0, 'glr': 256, 'sdt': 384}


def _regroup_w_in(w_in):
    ref = _ref_offsets()
    w_in = w_in.astype(BF16)
    lead = w_in.shape[:-1]
    parts = [w_in[..., ref[name][0]:ref[name][0] + ref[name][1]] for name in _PM_SEGS]
    for name, slot in _PF_SLOTS:
        o, w = ref[name]
        parts.append(w_in[..., o:o + w])
        if slot > w:
            parts.append(jnp.zeros(lead + (slot - w,), w_in.dtype))
    o, w = ref['mg']
    parts.append(w_in[..., o:o + w])
    out = jnp.concatenate(parts, axis=-1)
    assert out.shape[-1] == N_PROJ
    return out


def _level_matrices():
    t = np.arange(CHUNK)[:, None]
    s = np.arange(CHUNK)[None, :]
    blocks = [(s <= t), (s > t)]
    for m in LEVELS:
        r = (t // (2 * m)) * (2 * m) + m - 1
        blocks.append(np.where(t > r, (s > r) & (s <= t), (s > t) & (s <= r)))
    a = np.concatenate(blocks, axis=0).astype(np.float32)
    return np.concatenate([a, a, a], axis=1)


def _level_masks():
    t = np.arange(CHUNK)[:, None]
    s = np.tile(np.arange(CHUNK), N_HEADS)[None, :]
    out = []
    for m in LEVELS:
        out.append(((t // (2 * m)) == (s // (2 * m))) & ((t % (2 * m)) >= m) & ((s % (2 * m)) < m))
    out.append(t == s)
    out.append(t >= s)
    return np.concatenate(out, axis=0).astype(np.float32)


def _conv_shift_matrix():
    sh = np.zeros((SSM_CONV * CHUNK, CONV_TAIL + CHUNK), np.float32)
    for j in range(SSM_CONV):
        sh[j * CHUNK + np.arange(CHUNK), CONV_TAIL - (SSM_CONV - 1) + j + np.arange(CHUNK)] = 1.0
    return sh


def _split3(x):
    hi = x.astype(BF16)
    r = x - hi.astype(F32)
    mid = r.astype(BF16)
    lo = (r - mid.astype(F32)).astype(BF16)
    return hi, mid, lo


def _dot(a, b):
    return jnp.dot(a, b, preferred_element_type=F32)


def _dot_nt(a, b):
    return lax.dot_general(a, b, (((1,), (1,)), ((), ())), preferred_element_type=F32)


def _dot_tn(a, b):
    return lax.dot_general(a, b, (((0,), (0,)), ((), ())), preferred_element_type=F32)


def _sigmoid(x):
    return 0.5 * jnp.tanh(0.5 * x) + 0.5


def _silu(x):
    hx = 0.5 * x
    return hx * jnp.tanh(hx) + hx


def _log_sigmoid(x):
    return jnp.minimum(x, 0.0) - jnp.log(1.0 + jnp.exp(-jnp.abs(x)))


def _softplus(x):
    return jnp.maximum(x, 0.0) + jnp.log(1.0 + jnp.exp(-jnp.abs(x)))


def _group_rms(o, width):
    parts = []
    for h in range(o.shape[-1] // width):
        oh = o[:, h * width:(h + 1) * width]
        ms = jnp.mean(oh * oh, axis=-1, keepdims=True)
        parts.append(oh * lax.rsqrt(ms + EPS))
    return jnp.concatenate(parts, axis=1)


def _store_blockdiag(dst_ref, x, n_blocks, lane_w, lane_of_block=None):
    for b in range(n_blocks):
        g = b if lane_of_block is None else lane_of_block(b)
        dst_ref[b * CHUNK:(b + 1) * CHUNK, g * lane_w:(g + 1) * lane_w] = x[:, g * lane_w:(g + 1) * lane_w]


def _rms_rows(x, w):
    ms = jnp.mean(x * x, axis=-1, keepdims=True)
    return (x * lax.rsqrt(ms + EPS)) * w


def _norm_kernel(x_ref, nw_ref, h_ref):
    h_ref[...] = _rms_rows(x_ref[...], nw_ref[...]).astype(BF16)


def _norm(x2, nw, tm):
    m = x2.shape[0]
    return pl.pallas_call(
        _norm_kernel,
        grid=(m // tm,),
        in_specs=[pl.BlockSpec((tm, D_MODEL), lambda i: (i, 0)), pl.BlockSpec((1, D_MODEL), lambda i: (0, 0))],
        out_specs=pl.BlockSpec((tm, D_MODEL), lambda i: (i, 0)),
        out_shape=jax.ShapeDtypeStruct((m, D_MODEL), BF16),
        compiler_params=pltpu.CompilerParams(dimension_semantics=("arbitrary",)),
        name="norm",
    )(x2, nw)


def _inproj_kernel(h_ref, w_ref, proj_ref, pf_ref):
    res = _dot(h_ref[...], w_ref[...])
    proj_ref[...] = res.astype(BF16)

    @pl.when(pl.program_id(1) == PF_BLOCK)
    def _():
        pf_ref[...] = res[:, PROJ_TN - PF_W:]


def _inproj(h, w, tm):
    m = h.shape[0]
    return pl.pallas_call(
        _inproj_kernel,
        grid=(m // tm, N_PROJ // PROJ_TN),
        in_specs=[
            pl.BlockSpec((tm, D_MODEL), lambda i, j: (i, 0)),
            pl.BlockSpec((D_MODEL, PROJ_TN), lambda i, j: (0, j)),
        ],
        out_specs=[
            pl.BlockSpec((tm, PROJ_TN), lambda i, j: (i, j)),
            pl.BlockSpec((tm, PF_W), lambda i, j: (i, 0)),
        ],
        out_shape=[
            jax.ShapeDtypeStruct((m, N_PROJ), BF16),
            jax.ShapeDtypeStruct((m, PF_W), F32),
        ],
        compiler_params=pltpu.CompilerParams(
            dimension_semantics=("arbitrary", "arbitrary"),
            vmem_limit_bytes=VMEM_LIMIT),
        name="inproj",
    )(h, w)


def _gla_branch(prologue, finish, amat, vmask_ref, j256, s_ref, sv_ref, kbd_ref, kbdt_ref, par, fast):
    q_bf, k_bf, v_bf, g2 = yield from prologue()
    hi, mid, lo = _split3(g2)
    n_rows = 3 * CHUNK if fast else amat.shape[0]
    d = _dot(amat[0:n_rows], jnp.concatenate([hi, mid, lo], axis=0))
    yield
    e = jnp.exp2(d)
    e_bf = e.astype(BF16)
    qe = q_bf * e_bf[0:CHUNK]
    if fast:
        half = CHUNK // 2
        d_mid = d[2 * CHUNK:3 * CHUNK]
        near = e_bf[2 * CHUNK:3 * CHUNK]
        far = jnp.exp2(-d_mid).astype(BF16)
        qs = q_bf * jnp.concatenate([far[0:half], near[half:CHUNK]], axis=0)
        ks = k_bf * jnp.concatenate([near[0:half], far[half:CHUNK]], axis=0)
        blank = jnp.zeros((half, QK_W), BF16)
        lhs = jnp.concatenate([jnp.concatenate([blank, qs[half:CHUNK]], axis=0),
                               jnp.concatenate([qs[0:half], blank], axis=0)], axis=1)
        _store_blockdiag(kbd_ref.at[0:QK_W], ks, N_HEADS, DK)
        _store_blockdiag(kbd_ref.at[QK_W:2 * QK_W], jnp.concatenate([ks[0:half], blank], axis=0), N_HEADS, DK)
        kbdt_ref[0:QK_W, :] = kbd_ref[0:QK_W, :].T
        kbdt_ref[QK_W:2 * QK_W, :] = kbd_ref[QK_W:2 * QK_W, :].T
        sc = _dot(lhs, kbdt_ref[0:2 * QK_W, :])
        yield
        scores = sc * vmask_ref[(N_LEVELS + 1) * CHUNK:(N_LEVELS + 2) * CHUNK, :]
    else:
        scores = _dot(q_bf * k_bf, j256)
        yield
        scores = scores * vmask_ref[N_LEVELS * CHUNK:(N_LEVELS + 1) * CHUNK, :]
        pending = []
        for li in range(N_LEVELS):
            el = e_bf[(2 + li) * CHUNK:(3 + li) * CHUNK]
            lv = slice(li * QK_W, (li + 1) * QK_W)
            _store_blockdiag(kbd_ref.at[lv], k_bf * el, N_HEADS, DK)
            kbdt_ref[lv, :] = kbd_ref[lv, :].T
            pending.append((li, _dot(q_bf * el, kbdt_ref[lv, :])))
            if li % 2 == 1:
                yield
                for lj, sc in pending:
                    scores = scores + sc * vmask_ref[lj * CHUNK:(lj + 1) * CHUNK, :]
                pending = []
    _store_blockdiag(sv_ref.at[par, QK_W:2 * QK_W], v_bf, N_HEADS, DV)
    o = _dot(jnp.concatenate([qe, scores.astype(BF16)], axis=1), sv_ref[par])
    upd = _dot_tn(k_bf * e_bf[CHUNK:2 * CHUNK], v_bf)
    dcol = jnp.broadcast_to(e[CHUNK - 1:CHUNK, :], (DV, QK_W)).T
    yield
    for h in range(N_HEADS):
        r = slice(h * DK, (h + 1) * DK)
        c = slice(h * DV, (h + 1) * DV)
        new = s_ref[r, :] * dcol[r, :] + upd[r, c]
        s_ref[r, :] = new
        sv_ref[1 - par, r, c] = new.astype(BF16)
    finish(o)


def _ret_branch(prologue, finish, dm_ref, qdec_ref, kdec_ref, cdec, s_ref, sv_ref, kbd_ref, par):
    q_bf, k_bf, v_bf = yield from prologue()
    _store_blockdiag(kbd_ref, k_bf, N_HEADS, DK)
    scores = _dot_nt(q_bf, kbd_ref[...])
    _store_blockdiag(sv_ref.at[par, QK_W:2 * QK_W], v_bf, N_HEADS, DV)
    yield
    o = _dot(jnp.concatenate([q_bf * qdec_ref[...], (scores * dm_ref[...]).astype(BF16)], axis=1), sv_ref[par])
    upd = _dot_tn(k_bf * kdec_ref[...], v_bf)
    yield
    for h in range(N_HEADS):
        r = slice(h * DK, (h + 1) * DK)
        c = slice(h * DV, (h + 1) * DV)
        new = s_ref[r, :] * cdec[h] + upd[r, c]
        s_ref[r, :] = new
        sv_ref[1 - par, r, c] = new.astype(BF16)
    finish(o)


def _ssd_branch(prologue, finish, a2_row, d_row, tri3_ref, triones_ref, ntrit_ref, caus_ref, gmask_ref,
                ss_ref, ssbd_ref, xbd_ref, bbd_ref, par):
    heads_per_group = SSM_HEADS // SSM_GROUPS
    xs, dtx, bm_bf, cm_bf = yield from prologue()
    adt = dtx * a2_row
    hi, mid, lo = _split3(adt)
    ntrit = ntrit_ref[...]
    acol = _dot(tri3_ref[...], jnp.concatenate([hi, mid, lo], axis=0))
    aseg = _dot(triones_ref[...], jnp.concatenate([hi, mid, lo, hi * ntrit, mid * ntrit, lo * ntrit], axis=0))
    _store_blockdiag(bbd_ref, bm_bf, SSM_HEADS, SSM_N, lambda h: h // heads_per_group)
    cb = _dot_nt(cm_bf, bbd_ref[...])
    yoff = _dot(cm_bf, ssbd_ref[par])
    xdt = xs * dtx
    _store_blockdiag(xbd_ref, xdt.astype(BF16), SSM_HEADS, SSM_P)
    yield
    decay = jnp.exp2(jnp.minimum(aseg, 0.0)) * caus_ref[...]
    y = _dot((cb * decay).astype(BF16), xbd_ref[...])
    alast = acol[CHUNK - 1:CHUNK, :]
    upd = _dot_tn(bm_bf, (xdt * jnp.exp2(alast - acol)).astype(BF16))
    yield
    y = (jnp.exp2(acol) * yoff + d_row * xs) + y
    new = ss_ref[...] * jnp.exp2(alast) + upd * gmask_ref[...]
    ss_ref[...] = new
    ssbd_ref[1 - par] = new.astype(BF16)
    finish(y)


def _round_robin(gens):
    live = list(gens)
    while live:
        still = []
        for g in live:
            try:
                next(g)
                still.append(g)
            except StopIteration:
                pass
        live = still


def _mixer_kernel(pm_ref, pf_ref, cos_ref, sin_ref,
                  wlr_ref, blr_ref, gnorm_ref, lbl_ref, hnorm_ref,
                  convw_ref, convb_ref, dtb_ref, alog_ref, dsk_ref, snorm_ref, rnorm_ref,
                  amat_ref, vmask_ref, j256_ref, dm_ref, qdec_ref, kdec_ref,
                  tri3_ref, triones_ref, ntrit_ref, caus_ref, gmask_ref, dtexp_ref, lanelo_ref, shift_ref,
                  out_ref,
                  sa_ref, sva_ref, sb_ref, svb_ref, sd_ref, svd_ref, ss_ref, ssbd_ref,
                  kbda_ref, kbdb_ref, kbdd_ref, xbd_ref, bbd_ref, xc_ref,
                  qrot_ref, krot_ref, kbdta_ref, kbdtb_ref, g2a_ref, dtx_ref, g2b_ref, kb_ref,
                  *, layer, tile, cdec):
    i = pl.program_id(1)

    @pl.when(i == 0)
    def _():
        for r in (sa_ref, sva_ref, sb_ref, svb_ref, sd_ref, svd_ref, ss_ref, ssbd_ref,
                  kbda_ref, kbdb_ref, kbdd_ref, xbd_ref, bbd_ref):
            r[...] = jnp.zeros(r.shape, r.dtype)
        xc_ref[0, 0:CONV_TAIL, :] = jnp.zeros((CONV_TAIL, SSM_XBC), xc_ref.dtype)

    lbl = lbl_ref[...]
    ex = jnp.exp(lbl - jnp.max(lbl, axis=0, keepdims=True))
    lb = jnp.zeros((1, QK_W), F32)
    for l in range(layer):
        lb = lb + ex[l:l + 1, :]
    lb_floor = jnp.maximum(lb / jnp.sum(ex, axis=0, keepdims=True), LB_FLOOR)
    log_lb = jnp.log(lb_floor)

    amat = amat_ref[...]
    j256 = j256_ref[...]
    a2_row = -jnp.exp(alog_ref[...]) * LOG2E

    lr = pf_ref[:, PF_OFF['glr']:PF_OFF['glr'] + 128].astype(BF16)
    g2a_ref[...] = _log_sigmoid(_dot(lr, wlr_ref[...]) + blr_ref[...]) * (LOG2E / GLA_TAU)
    dth, dtm, dtl = _split3(pf_ref[:, PF_OFF['sdt']:PF_OFF['sdt'] + 128])
    dtx_ref[...] = _softplus(_dot(jnp.concatenate([dth, dtm, dtl], axis=1), dtexp_ref[...]) + dtb_ref[...])

    z = pf_ref[:, PF_OFF['hf']:PF_OFF['hf'] + QK_W]
    la = _log_sigmoid(z)
    lsn = la - z
    g2b_ref[...] = (jnp.maximum(la, log_lb + lsn) + jnp.log(1.0 + jnp.exp(-jnp.abs(z - log_lb)))) * LOG2E
    kb_ref[...] = ((1.0 - lb_floor) * jnp.exp(lsn)).astype(BF16)

    half = CHUNK // 2
    worst = None
    for g_ref in (g2a_ref, g2b_ref):
        for c in range(tile // half):
            s = jnp.sum(g_ref[c * half:(c + 1) * half, :], axis=0, keepdims=True)
            worst = s if worst is None else jnp.minimum(worst, s)
    safe = jnp.min(worst) >= -SAFE_LOG2

    lanelo = lanelo_ref[...]
    for c in range(tile // CHUNK):
        rws = slice(c * CHUNK, (c + 1) * CHUNK)
        cos = cos_ref[rws, :]
        sin = sin_ref[rws, :]
        for name, dst, scale in (('rq', qrot_ref, 1.0), ('rk', krot_ref, DK ** -0.5)):
            t = pm_ref[rws, PM_OFF[name]:PM_OFF[name] + QK_W].astype(F32)
            swapped = jnp.where(lanelo > 0.5, pltpu.roll(t, QK_W - DK // 2, 1), pltpu.roll(t, DK // 2, 1))
            dst[rws, :] = ((t * cos + swapped * sin) * scale).astype(BF16)

    def chunk_branches(c, fast):
        r0 = pl.multiple_of(c * CHUNK, CHUNK)
        rows = pl.ds(r0, CHUNK)
        par = lax.rem(c, 2)

        def seg(name, w):
            return pm_ref[rows, PM_OFF[name]:PM_OFF[name] + w]

        def finish_gated(col, norm_ref, gate_name):
            def finish(o):
                o = _group_rms(o, DV) * norm_ref[...] * _silu(seg(gate_name, BRANCH_W).astype(F32))
                out_ref[rows, col * BRANCH_W:(col + 1) * BRANCH_W] = o.astype(BF16)
            return finish

        def gla_prologue():
            return seg('gq', QK_W) * (DK ** -0.5), seg('gk', QK_W), seg('gv', BRANCH_W), g2a_ref[rows, :]
            yield

        def hgrn_prologue():
            return seg('hq', QK_W), kb_ref[rows, :], seg('hi', BRANCH_W), g2b_ref[rows, :]
            yield

        def ssd_prologue():
            cur = seg('sxbc', SSM_XBC)
            xc_ref[par, CONV_TAIL:CONV_TAIL + CHUNK, :] = cur
            taps = _dot(shift_ref[...], xc_ref[par])
            xc_ref[1 - par, 0:CONV_TAIL, :] = cur[CHUNK - CONV_TAIL:CHUNK, :]
            yield
            acc = jnp.broadcast_to(convb_ref[...], (CHUNK, SSM_XBC))
            for j in range(SSM_CONV):
                acc = acc + taps[j * CHUNK:(j + 1) * CHUNK] * convw_ref[j:j + 1, :]
            xbc = _silu(acc)
            return (xbc[:, 0:BRANCH_W], dtx_ref[rows, :], xbc[:, BRANCH_W:BRANCH_W + 128].astype(BF16),
                    xbc[:, BRANCH_W + 128:].astype(BF16))

        def ssd_finish(y):
            y = y * _silu(seg('sz', BRANCH_W).astype(F32))
            y = _group_rms(y, BRANCH_W // SSM_GROUPS) * snorm_ref[...]
            out_ref[rows, 2 * BRANCH_W:3 * BRANCH_W] = y.astype(BF16)

        def ret_prologue():
            return qrot_ref[rows, :], krot_ref[rows, :], seg('rv', BRANCH_W)
            yield

        return [
            _gla_branch(gla_prologue, finish_gated(0, gnorm_ref, 'gg'), amat, vmask_ref, j256,
                        sa_ref, sva_ref, kbda_ref, kbdta_ref, par, fast),
            _gla_branch(hgrn_prologue, finish_gated(1, hnorm_ref, 'hg'), amat, vmask_ref, j256,
                        sb_ref, svb_ref, kbdb_ref, kbdtb_ref, par, fast),
            _ssd_branch(ssd_prologue, ssd_finish, a2_row, dsk_ref[...], tri3_ref, triones_ref, ntrit_ref, caus_ref,
                        gmask_ref, ss_ref, ssbd_ref, xbd_ref, bbd_ref, par),
            _ret_branch(ret_prologue, finish_gated(3, rnorm_ref, 'rg'), dm_ref, qdec_ref, kdec_ref, cdec,
                        sd_ref, svd_ref, kbdd_ref, par),
        ]

    def delayed(gen, n):
        for _ in range(n):
            yield
        yield from gen

    def run_chunks(fast):
        def group_body(p, carry):
            gens = []
            for u in range(PIPE_CHUNKS):
                gens += [delayed(g, PIPE_SKEW * u) for g in chunk_branches(p * PIPE_CHUNKS + u, fast)]
            _round_robin(gens)
            return carry

        lax.fori_loop(0, tile // (CHUNK * PIPE_CHUNKS), group_body, 0)

    @pl.when(safe)
    def _():
        run_chunks(True)

    @pl.when(jnp.logical_not(safe))
    def _():
        run_chunks(False)


def _const_spec(a):
    nd = a.ndim
    return pl.BlockSpec(a.shape, lambda b, i: (0,) * nd)


def _mixer(pm, pf, cos_t, sin_t, params, consts, *, batch, seq, layer, tile, cdec):
    n_t = seq // tile
    row_map = lambda b, i: (b * n_t + i, 0)
    small = list(params) + list(consts)
    in_specs = [
        pl.BlockSpec((tile, PM_W), row_map),
        pl.BlockSpec((tile, PF_W), row_map),
        pl.BlockSpec((tile, QK_W), lambda b, i: (i, 0)),
        pl.BlockSpec((tile, QK_W), lambda b, i: (i, 0)),
    ] + [_const_spec(a) for a in small]
    state = lambda: [pltpu.VMEM((QK_W, DV), F32), pltpu.VMEM((2, 2 * QK_W, BRANCH_W), BF16)]
    return pl.pallas_call(
        functools.partial(_mixer_kernel, layer=layer, tile=tile, cdec=cdec),
        grid=(batch, n_t),
        in_specs=in_specs,
        out_specs=pl.BlockSpec((tile, 4 * BRANCH_W), row_map),
        out_shape=jax.ShapeDtypeStruct((batch * seq, 4 * BRANCH_W), BF16),
        scratch_shapes=state() + state() + state() + [
            pltpu.VMEM((SSM_GROUPS * SSM_N, BRANCH_W), F32), pltpu.VMEM((2, SSM_GROUPS * SSM_N, BRANCH_W), BF16),
            pltpu.VMEM((N_LEVELS * QK_W, QK_W), BF16), pltpu.VMEM((N_LEVELS * QK_W, QK_W), BF16),
            pltpu.VMEM((QK_W, QK_W), BF16),
            pltpu.VMEM((SSM_HEADS * CHUNK, BRANCH_W), BF16), pltpu.VMEM((SSM_HEADS * CHUNK, SSM_GROUPS * SSM_N), BF16),
            pltpu.VMEM((2, CONV_TAIL + CHUNK, SSM_XBC), BF16),
            pltpu.VMEM((tile, QK_W), BF16), pltpu.VMEM((tile, QK_W), BF16),
            pltpu.VMEM((N_LEVELS * QK_W, QK_W), BF16), pltpu.VMEM((N_LEVELS * QK_W, QK_W), BF16),
            pltpu.VMEM((tile, QK_W), F32), pltpu.VMEM((tile, BRANCH_W), F32),
            pltpu.VMEM((tile, QK_W), F32), pltpu.VMEM((tile, QK_W), BF16),
        ],
        compiler_params=pltpu.CompilerParams(
            dimension_semantics=("arbitrary", "arbitrary"),
            vmem_limit_bytes=VMEM_LIMIT),
        name="mixer",
    )(pm, pf, cos_t, sin_t, *small)


def _merge_kernel(o_ref, mga_ref, mgb_ref, x_ref, wup_ref, wout_ref, nw_ref, *out_refs, final):
    merged = None
    for n in range(4):
        up = _dot(o_ref[:, n * BRANCH_W:(n + 1) * BRANCH_W], wup_ref[n])
        mg_ref = mga_ref if n < 2 else mgb_ref
        gate = _sigmoid(mg_ref[:, (n % 2) * D_MODEL:(n % 2 + 1) * D_MODEL].astype(F32))
        merged = gate * up if merged is None else merged + gate * up
    y = x_ref[...] + _dot(merged.astype(BF16), wout_ref[...])
    yn = _rms_rows(y, nw_ref[...])
    if final:
        out_refs[0][...] = yn
    else:
        out_refs[0][...] = y
        out_refs[1][...] = yn.astype(BF16)


def _merge(outs, proj, x2, wup, wout, nw, *, tm, final):
    m = x2.shape[0]
    row = lambda i: (i, 0)
    if final:
        out_specs = [pl.BlockSpec((tm, D_MODEL), row)]
        out_shape = [jax.ShapeDtypeStruct((m, D_MODEL), F32)]
    else:
        out_specs = [pl.BlockSpec((tm, D_MODEL), row), pl.BlockSpec((tm, D_MODEL), row)]
        out_shape = [jax.ShapeDtypeStruct((m, D_MODEL), F32), jax.ShapeDtypeStruct((m, D_MODEL), BF16)]
    return pl.pallas_call(
        functools.partial(_merge_kernel, final=final),
        grid=(m // tm,),
        in_specs=[
            pl.BlockSpec((tm, 4 * BRANCH_W), row),
            pl.BlockSpec((tm, MG_HALF), lambda i: (i, MG_OFF // MG_HALF)),
            pl.BlockSpec((tm, MG_HALF), lambda i: (i, MG_OFF // MG_HALF + 1)),
            pl.BlockSpec((tm, D_MODEL), row),
            pl.BlockSpec((4, BRANCH_W, D_MODEL), lambda i: (0, 0, 0)),
            pl.BlockSpec((D_MODEL, D_MODEL), lambda i: (0, 0)),
            pl.BlockSpec((1, D_MODEL), lambda i: (0, 0)),
        ],
        out_specs=out_specs,
        out_shape=out_shape,
        compiler_params=pltpu.CompilerParams(
            dimension_semantics=("arbitrary",),
            vmem_limit_bytes=VMEM_LIMIT),
        name="merge",
    )(outs, proj, proj, x2, wup, wout, nw)


def _tile_rows(n, pref):
    t = min(pref, n)
    assert n % t == 0
    return t


def kernel(x, norm_w, w_in, gla_w_lr, gla_b_lr, gla_norm, hg_lb_logits, hg_norm, ssm_conv_w, ssm_conv_b,
           ssm_dt_bias, ssm_a_log, ssm_d, ssm_norm, ret_norm, w_up, w_out, final_norm):
    batch, seq, d = x.shape
    depth = norm_w.shape[0]
    assert d == D_MODEL and seq % CHUNK == 0
    m = batch * seq
    tile = _tile_rows(seq, 512)
    assert tile % (CHUNK * PIPE_CHUNKS) == 0 and (tile // CHUNK) % 2 == 0 and PIPE_SKEW >= 1
    tm_proj = _tile_rows(m, 1024)
    tm_merge = _tile_rows(m, 512)

    w_in_r = _regroup_w_in(w_in)
    w_up_b = w_up.astype(BF16)
    w_out_b = w_out.astype(BF16)
    wlr = jnp.zeros((depth, 128, QK_W), F32).at[:, :GLA_RANK, :].set(gla_w_lr).astype(BF16)

    inv = 1.0 / (ROPE_BASE ** (jnp.arange(0, DK, 2, dtype=F32) / DK))
    ang = jnp.arange(seq, dtype=F32)[:, None] * inv[None, :]
    cos_t = jnp.tile(jnp.cos(ang), (1, 2 * N_HEADS))
    sin_t = jnp.tile(jnp.concatenate([-jnp.sin(ang), jnp.sin(ang)], axis=1), (1, N_HEADS))

    log_gamma = jnp.log(1.0 - 2.0 ** (-5.0 - jnp.arange(N_HEADS, dtype=F32)))
    lg_k = jnp.repeat(log_gamma, DK)[None, :]
    tpos = jnp.arange(CHUNK, dtype=F32)[:, None]
    spos = jnp.tile(jnp.arange(CHUNK, dtype=F32), N_HEADS)[None, :]
    dm = jnp.exp(lg_k * jnp.abs(tpos - spos))
    qdec = jnp.exp(lg_k * (tpos + 1.0)).astype(BF16)
    kdec = jnp.exp(lg_k * (CHUNK - 1.0 - tpos)).astype(BF16)
    cdec_np = np.exp(np.log(1.0 - 2.0 ** (-5.0 - np.arange(N_HEADS))) * CHUNK)
    cdec = tuple(float(c) for c in cdec_np)

    tri = np.tril(np.ones((CHUNK, CHUNK), np.float32))
    s_of_lane = np.tile(np.arange(CHUNK), SSM_HEADS)[None, :]
    srow = np.arange(CHUNK)[:, None]
    dtexp = np.zeros((128, BRANCH_W), np.float32)
    dtexp[np.arange(BRANCH_W) // SSM_P, np.arange(BRANCH_W)] = 1.0
    heads_per_group = SSM_HEADS // SSM_GROUPS
    gmask = (np.arange(SSM_GROUPS * SSM_N)[:, None] // SSM_N
             == np.arange(BRANCH_W)[None, :] // (SSM_P * heads_per_group)).astype(np.float32)
    consts = [
        jnp.asarray(_level_matrices(), BF16),
        jnp.asarray(_level_masks(), F32),
        jnp.asarray(np.kron(np.eye(N_HEADS, dtype=np.float32), np.ones((DK, CHUNK), np.float32)), BF16),
        dm, qdec, kdec,
        jnp.asarray(np.concatenate([tri, tri, tri], axis=1), BF16),
        jnp.asarray(np.concatenate([tri, tri, tri, np.ones((CHUNK, 3 * CHUNK), np.float32)], axis=1), BF16),
        jnp.asarray(-(srow <= s_of_lane).astype(np.float32), BF16),
        jnp.asarray((srow >= s_of_lane).astype(np.float32)),
        jnp.asarray(gmask),
        jnp.asarray(np.concatenate([dtexp, dtexp, dtexp], axis=0), BF16),
        jnp.asarray((np.arange(QK_W)[None, :] % DK < DK // 2).astype(np.float32)),
        jnp.asarray(_conv_shift_matrix(), BF16),
    ]

    x2 = x.reshape(m, D_MODEL)
    h = _norm(x2, norm_w[0][None, :], tm_proj)
    for l in range(depth):
        proj, pf = _inproj(h, w_in_r[l], tm_proj)
        params = [
            wlr[l], gla_b_lr[l][None, :], jnp.tile(gla_norm[l], N_HEADS)[None, :],
            hg_lb_logits, jnp.tile(hg_norm[l], N_HEADS)[None, :],
            ssm_conv_w[l], ssm_conv_b[l][None, :],
            jnp.repeat(ssm_dt_bias[l], SSM_P)[None, :], jnp.repeat(ssm_a_log[l], SSM_P)[None, :],
            jnp.repeat(ssm_d[l], SSM_P)[None, :], ssm_norm[l][None, :],
            jnp.tile(ret_norm[l], N_HEADS)[None, :],
        ]
        outs = _mixer(proj, pf, cos_t, sin_t, params, consts, batch=batch, seq=seq, layer=l, tile=tile, cdec=cdec)
        if l == depth - 1:
            (x2,) = _merge(outs, proj, x2, w_up_b[l], w_out_b[l], final_norm[None, :], tm=tm_merge, final=True)
        else:
            x2, h = _merge(outs, proj, x2, w_up_b[l], w_out_b[l], norm_w[l + 1][None, :], tm=tm_merge, final=False)
    return x2.reshape(batch, seq, D_MODEL)
```

```python
import functools
import math

import numpy as np
import jax
import jax.numpy as jnp
from jax import lax
from jax.experimental import pallas as pl
from jax.experimental.pallas import tpu as pltpu

F32 = jnp.float32
BF16 = jnp.bfloat16

D_MODEL = 1024
CHUNK = 64
EPS = 1e-6
LB_FLOOR = 1e-30
BRANCH_W = 512
N_HEADS = 4
DK = 64
DV = 128
QK_W = N_HEADS * DK
GLA_RANK = 16
GLA_TAU = 16.0
SSM_HEADS = 8
SSM_P = 64
SSM_GROUPS = 2
SSM_N = 64
SSM_CONV = 4
SSM_XBC = BRANCH_W + 2 * SSM_GROUPS * SSM_N
ROPE_BASE = 10000.0
LEVELS = (32, 16, 8, 4, 2, 1)
N_LEVELS = len(LEVELS)
LOG2E = math.log2(math.e)
CONV_TAIL = 16
PIPE_CHUNKS = 2
PIPE_SKEW = 2
SAFE_LOG2 = 100.0

_REF_SEGS = (('gq', 256), ('gk', 256), ('gv', 512), ('glr', 16), ('gg', 512),
             ('hq', 256), ('hf', 256), ('hi', 512), ('hg', 512),
             ('sz', 512), ('sxbc', 768), ('sdt', 8),
             ('rq', 256), ('rk', 256), ('rv', 512), ('rg', 512), ('mg', 4096))
_PM_SEGS = ('gq', 'gk', 'gv', 'gg', 'hq', 'hi', 'hg', 'sz', 'sxbc', 'rq', 'rk', 'rv', 'rg')
_PF_SLOTS = (('hf', 256), ('glr', 128), ('sdt', 128))
PM_W = 5632
PF_W = 512
MG_W = 4096
N_PROJ = PM_W + PF_W + MG_W
PROJ_TN = 1024
PF_BLOCK = (PM_W + PF_W) // PROJ_TN - 1
MG_OFF = PM_W + PF_W
MG_HALF = MG_W // 2
VMEM_LIMIT = 48 * 1024 * 1024


def _ref_offsets():
    off, out = 0, {}
    for name, w in _REF_SEGS:
        out[name] = (off, w)
        off += w
    return out


def _pm_offsets():
    ref = _ref_offsets()
    off, out = 0, {}
    for name in _PM_SEGS:
        out[name] = off
        off += ref[name][1]
    assert off == PM_W
    return out


PM_OFF = _pm_offsets()
PF_OFF = {'hf': 0, 'glr': 256, 'sdt': 384}


def _regroup_w_in(w_in):
    ref = _ref_offsets()
    w_in = w_in.astype(BF16)
    lead = w_in.shape[:-1]
    runs = []
    for name in _PM_SEGS:
        o, w = ref[name]
        if runs and runs[-1][1] == o:
            runs[-1][1] = o + w
        else:
            runs.append([o, o + w])
    parts = [w_in[..., a:b] for a, b in runs]
    for name, slot in _PF_SLOTS:
        o, w = ref[name]
        parts.append(w_in[..., o:o + w])
        if slot > w:
            parts.append(jnp.zeros(lead + (slot - w,), w_in.dtype))
    o, w = ref['mg']
    parts.append(w_in[..., o:o + w])
    out = jnp.concatenate(parts, axis=-1)
    assert out.shape[-1] == N_PROJ
    return out


def _level_matrices():
    t = np.arange(CHUNK)[:, None]
    s = np.arange(CHUNK)[None, :]
    blocks = [(s <= t), (s > t)]
    for m in LEVELS:
        r = (t // (2 * m)) * (2 * m) + m - 1
        blocks.append(np.where(t > r, (s > r) & (s <= t), (s > t) & (s <= r)))
    a = np.concatenate(blocks, axis=0).astype(np.float32)
    return np.concatenate([a, a, a], axis=1)


def _level_masks():
    t = np.arange(CHUNK)[:, None]
    s = np.tile(np.arange(CHUNK), N_HEADS)[None, :]
    out = []
    for m in LEVELS:
        out.append(((t // (2 * m)) == (s // (2 * m))) & ((t % (2 * m)) >= m) & ((s % (2 * m)) < m))
    out.append(t == s)
    out.append(t >= s)
    return np.concatenate(out, axis=0).astype(np.float32)


def _conv_shift_matrix():
    sh = np.zeros((SSM_CONV * CHUNK, CONV_TAIL + CHUNK), np.float32)
    for j in range(SSM_CONV):
        sh[j * CHUNK + np.arange(CHUNK), CONV_TAIL - (SSM_CONV - 1) + j + np.arange(CHUNK)] = 1.0
    return sh


def _split3(x):
    hi = x.astype(BF16)
    r = x - hi.astype(F32)
    mid = r.astype(BF16)
    lo = (r - mid.astype(F32)).astype(BF16)
    return hi, mid, lo


def _dot(a, b):
    return jnp.dot(a, b, preferred_element_type=F32)


def _dot_nt(a, b):
    return lax.dot_general(a, b, (((1,), (1,)), ((), ())), preferred_element_type=F32)


def _dot_tn(a, b):
    return lax.dot_general(a, b, (((0,), (0,)), ((), ())), preferred_element_type=F32)


def _sigmoid(x):
    return 0.5 * jnp.tanh(0.5 * x) + 0.5


def _silu(x):
    hx = 0.5 * x
    return hx * jnp.tanh(hx) + hx


def _log_sigmoid(x):
    return jnp.minimum(x, 0.0) - jnp.log(1.0 + jnp.exp(-jnp.abs(x)))


def _softplus(x):
    return jnp.maximum(x, 0.0) + jnp.log(1.0 + jnp.exp(-jnp.abs(x)))


def _group_rms(o, width):
    parts = []
    for h in range(o.shape[-1] // width):
        oh = o[:, h * width:(h + 1) * width]
        ms = jnp.mean(oh * oh, axis=-1, keepdims=True)
        parts.append(oh * lax.rsqrt(ms + EPS))
    return jnp.concatenate(parts, axis=1)


def _store_blockdiag(dst_ref, x, n_blocks, lane_w, lane_of_block=None):
    for b in range(n_blocks):
        g = b if lane_of_block is None else lane_of_block(b)
        dst_ref[b * CHUNK:(b + 1) * CHUNK, g * lane_w:(g + 1) * lane_w] = x[:, g * lane_w:(g + 1) * lane_w]


def _rms_rows(x, w):
    ms = jnp.mean(x * x, axis=-1, keepdims=True)
    return (x * lax.rsqrt(ms + EPS)) * w


def _norm_kernel(x_ref, nw_ref, h_ref):
    h_ref[...] = _rms_rows(x_ref[...], nw_ref[...]).astype(BF16)


def _norm(x2, nw, tm):
    m = x2.shape[0]
    return pl.pallas_call(
        _norm_kernel,
        grid=(m // tm,),
        in_specs=[pl.BlockSpec((tm, D_MODEL), lambda i: (i, 0)), pl.BlockSpec((1, D_MODEL), lambda i: (0, 0))],
        out_specs=pl.BlockSpec((tm, D_MODEL), lambda i: (i, 0)),
        out_shape=jax.ShapeDtypeStruct((m, D_MODEL), BF16),
        compiler_params=pltpu.CompilerParams(dimension_semantics=("arbitrary",)),
        name="norm",
    )(x2, nw)


def _inproj_kernel(h_ref, w_ref, proj_ref, pf_ref):
    res = _dot(h_ref[...], w_ref[...])
    proj_ref[...] = res.astype(BF16)

    @pl.when(pl.program_id(1) == PF_BLOCK)
    def _():
        pf_ref[...] = res[:, PROJ_TN - PF_W:]


def _inproj(h, w, tm):
    m = h.shape[0]
    return pl.pallas_call(
        _inproj_kernel,
        grid=(m // tm, N_PROJ // PROJ_TN),
        in_specs=[
            pl.BlockSpec((tm, D_MODEL), lambda i, j: (i, 0)),
            pl.BlockSpec((D_MODEL, PROJ_TN), lambda i, j: (0, j)),
        ],
        out_specs=[
            pl.BlockSpec((tm, PROJ_TN), lambda i, j: (i, j)),
            pl.BlockSpec((tm, PF_W), lambda i, j: (i, 0)),
        ],
        out_shape=[
            jax.ShapeDtypeStruct((m, N_PROJ), BF16),
            jax.ShapeDtypeStruct((m, PF_W), F32),
        ],
        compiler_params=pltpu.CompilerParams(
            dimension_semantics=("arbitrary", "arbitrary"),
            vmem_limit_bytes=VMEM_LIMIT),
        name="inproj",
    )(h, w)


def _gla_branch(prologue, finish, amat, vmask_ref, j256, s_ref, sv_ref, kbd_ref, kbdt_ref, par, fast):
    q_bf, k_bf, v_bf, g2 = yield from prologue()
    hi, mid, lo = _split3(g2)
    n_rows = 3 * CHUNK if fast else amat.shape[0]
    d = _dot(amat[0:n_rows], jnp.concatenate([hi, mid, lo], axis=0))
    yield
    e = jnp.exp2(d)
    e_bf = e.astype(BF16)
    qe = q_bf * e_bf[0:CHUNK]
    if fast:
        half = CHUNK // 2
        d_mid = d[2 * CHUNK:3 * CHUNK]
        near = e_bf[2 * CHUNK:3 * CHUNK]
        far = jnp.exp2(-d_mid).astype(BF16)
        qs = q_bf * jnp.concatenate([far[0:half], near[half:CHUNK]], axis=0)
        ks = k_bf * jnp.concatenate([near[0:half], far[half:CHUNK]], axis=0)
        blank = jnp.zeros((half, QK_W), BF16)
        lhs = jnp.concatenate([jnp.concatenate([blank, qs[half:CHUNK]], axis=0),
                               jnp.concatenate([qs[0:half], blank], axis=0)], axis=1)
        _store_blockdiag(kbd_ref.at[0:QK_W], ks, N_HEADS, DK)
        _store_blockdiag(kbd_ref.at[QK_W:2 * QK_W], jnp.concatenate([ks[0:half], blank], axis=0), N_HEADS, DK)
        kbdt_ref[0:QK_W, :] = kbd_ref[0:QK_W, :].T
        kbdt_ref[QK_W:2 * QK_W, :] = kbd_ref[QK_W:2 * QK_W, :].T
        sc = _dot(lhs, kbdt_ref[0:2 * QK_W, :])
        yield
        scores = sc * vmask_ref[(N_LEVELS + 1) * CHUNK:(N_LEVELS + 2) * CHUNK, :]
    else:
        scores = _dot(q_bf * k_bf, j256)
        yield
        scores = scores * vmask_ref[N_LEVELS * CHUNK:(N_LEVELS + 1) * CHUNK, :]
        pending = []
        for li in range(N_LEVELS):
            el = e_bf[(2 + li) * CHUNK:(3 + li) * CHUNK]
            lv = slice(li * QK_W, (li + 1) * QK_W)
            _store_blockdiag(kbd_ref.at[lv], k_bf * el, N_HEADS, DK)
            kbdt_ref[lv, :] = kbd_ref[lv, :].T
            pending.append((li, _dot(q_bf * el, kbdt_ref[lv, :])))
            if li % 2 == 1:
                yield
                for lj, sc in pending:
                    scores = scores + sc * vmask_ref[lj * CHUNK:(lj + 1) * CHUNK, :]
                pending = []
    _store_blockdiag(sv_ref.at[par, QK_W:2 * QK_W], v_bf, N_HEADS, DV)
    o = _dot(jnp.concatenate([qe, scores.astype(BF16)], axis=1), sv_ref[par])
    upd = _dot_tn(k_bf * e_bf[CHUNK:2 * CHUNK], v_bf)
    dcol = jnp.broadcast_to(e[CHUNK - 1:CHUNK, :], (DV, QK_W)).T
    yield
    for h in range(N_HEADS):
        r = slice(h * DK, (h + 1) * DK)
        c = slice(h * DV, (h + 1) * DV)
        new = s_ref[r, :] * dcol[r, :] + upd[r, c]
        s_ref[r, :] = new
        sv_ref[1 - par, r, c] = new.astype(BF16)
    finish(o)


def _ret_branch(prologue, finish, dm_ref, qdec_ref, kdec_ref, cdec, s_ref, sv_ref, kbd_ref, par):
    q_bf, k_bf, v_bf = yield from prologue()
    _store_blockdiag(kbd_ref, k_bf, N_HEADS, DK)
    scores = _dot_nt(q_bf, kbd_ref[...])
    _store_blockdiag(sv_ref.at[par, QK_W:2 * QK_W], v_bf, N_HEADS, DV)
    yield
    o = _dot(jnp.concatenate([q_bf * qdec_ref[...], (scores * dm_ref[...]).astype(BF16)], axis=1), sv_ref[par])
    upd = _dot_tn(k_bf * kdec_ref[...], v_bf)
    yield
    for h in range(N_HEADS):
        r = slice(h * DK, (h + 1) * DK)
        c = slice(h * DV, (h + 1) * DV)
        new = s_ref[r, :] * cdec[h] + upd[r, c]
        s_ref[r, :] = new
        sv_ref[1 - par, r, c] = new.astype(BF16)
    finish(o)


def _ssd_branch(prologue, finish, a2_row, d_row, tri3_ref, triones_ref, ntrit_ref, caus_ref, gmask_ref,
                ss_ref, ssbd_ref, xbd_ref, bbd_ref, par):
    heads_per_group = SSM_HEADS // SSM_GROUPS
    xs, dtx, bm_bf, cm_bf = yield from prologue()
    adt = dtx * a2_row
    hi, mid, lo = _split3(adt)
    ntrit = ntrit_ref[...]
    acol = _dot(tri3_ref[...], jnp.concatenate([hi, mid, lo], axis=0))
    aseg = _dot(triones_ref[...], jnp.concatenate([hi, mid, lo, hi * ntrit, mid * ntrit, lo * ntrit], axis=0))
    _store_blockdiag(bbd_ref, bm_bf, SSM_HEADS, SSM_N, lambda h: h // heads_per_group)
    cb = _dot_nt(cm_bf, bbd_ref[...])
    yoff = _dot(cm_bf, ssbd_ref[par])
    xdt = xs * dtx
    _store_blockdiag(xbd_ref, xdt.astype(BF16), SSM_HEADS, SSM_P)
    yield
    decay = jnp.exp2(jnp.minimum(aseg, 0.0)) * caus_ref[...]
    y = _dot((cb * decay).astype(BF16), xbd_ref[...])
    alast = acol[CHUNK - 1:CHUNK, :]
    upd = _dot_tn(bm_bf, (xdt * jnp.exp2(alast - acol)).astype(BF16))
    yield
    y = (jnp.exp2(acol) * yoff + d_row * xs) + y
    new = ss_ref[...] * jnp.exp2(alast) + upd * gmask_ref[...]
    ss_ref[...] = new
    ssbd_ref[1 - par] = new.astype(BF16)
    finish(y)


def _round_robin(gens):
    live = list(gens)
    while live:
        still = []
        for g in live:
            try:
                next(g)
                still.append(g)
            except StopIteration:
                pass
        live = still


def _mixer_kernel(pm_ref, pf_ref, cos_ref, sin_ref,
                  wlr_ref, blr_ref, gnorm_ref, lbl_ref, hnorm_ref,
                  convw_ref, convb_ref, dtb_ref, alog_ref, dsk_ref, snorm_ref, rnorm_ref,
                  amat_ref, vmask_ref, j256_ref, dm_ref, qdec_ref, kdec_ref,
                  tri3_ref, triones_ref, ntrit_ref, caus_ref, gmask_ref, dtexp_ref, lanelo_ref, shift_ref,
                  out_ref,
                  sa_ref, sva_ref, sb_ref, svb_ref, sd_ref, svd_ref, ss_ref, ssbd_ref,
                  kbda_ref, kbdb_ref, kbdd_ref, xbd_ref, bbd_ref, xc_ref,
                  qrot_ref, krot_ref, kbdta_ref, kbdtb_ref, g2a_ref, dtx_ref, g2b_ref, kb_ref,
                  *, layer, tile, cdec):
    i = pl.program_id(1)

    @pl.when(i == 0)
    def _():
        for r in (sa_ref, sva_ref, sb_ref, svb_ref, sd_ref, svd_ref, ss_ref, ssbd_ref,
                  kbda_ref, kbdb_ref, kbdd_ref, xbd_ref, bbd_ref):
            r[...] = jnp.zeros(r.shape, r.dtype)
        xc_ref[0, 0:CONV_TAIL, :] = jnp.zeros((CONV_TAIL, SSM_XBC), xc_ref.dtype)

    lbl = lbl_ref[...]
    ex = jnp.exp(lbl - jnp.max(lbl, axis=0, keepdims=True))
    lb = jnp.zeros((1, QK_W), F32)
    for l in range(layer):
        lb = lb + ex[l:l + 1, :]
    lb_floor = jnp.maximum(lb / jnp.sum(ex, axis=0, keepdims=True), LB_FLOOR)
    log_lb = jnp.log(lb_floor)

    amat = amat_ref[...]
    j256 = j256_ref[...]
    a2_row = -jnp.exp(alog_ref[...]) * LOG2E

    lr = pf_ref[:, PF_OFF['glr']:PF_OFF['glr'] + 128].astype(BF16)
    g2a_ref[...] = _log_sigmoid(_dot(lr, wlr_ref[...]) + blr_ref[...]) * (LOG2E / GLA_TAU)
    dth, dtm, dtl = _split3(pf_ref[:, PF_OFF['sdt']:PF_OFF['sdt'] + 128])
    dtx_ref[...] = _softplus(_dot(jnp.concatenate([dth, dtm, dtl], axis=1), dtexp_ref[...]) + dtb_ref[...])

    z = pf_ref[:, PF_OFF['hf']:PF_OFF['hf'] + QK_W]
    la = _log_sigmoid(z)
    lsn = la - z
    g2b_ref[...] = (jnp.maximum(la, log_lb + lsn) + jnp.log(1.0 + jnp.exp(-jnp.abs(z - log_lb)))) * LOG2E
    kb_ref[...] = ((1.0 - lb_floor) * jnp.exp(lsn)).astype(BF16)

    half = CHUNK // 2
    worst = None
    for g_ref in (g2a_ref, g2b_ref):
        for c in range(tile // half):
            s = jnp.sum(g_ref[c * half:(c + 1) * half, :], axis=0, keepdims=True)
            worst = s if worst is None else jnp.minimum(worst, s)
    safe = jnp.min(worst) >= -SAFE_LOG2

    lanelo = lanelo_ref[...]
    for c in range(tile // CHUNK):
        rws = slice(c * CHUNK, (c + 1) * CHUNK)
        cos = cos_ref[rws, :]
        sin = sin_ref[rws, :]
        for name, dst, scale in (('rq', qrot_ref, 1.0), ('rk', krot_ref, DK ** -0.5)):
            t = pm_ref[rws, PM_OFF[name]:PM_OFF[name] + QK_W].astype(F32)
            swapped = jnp.where(lanelo > 0.5, pltpu.roll(t, QK_W - DK // 2, 1), pltpu.roll(t, DK // 2, 1))
            dst[rws, :] = ((t * cos + swapped * sin) * scale).astype(BF16)

    def chunk_branches(c, fast, par):
        r0 = pl.multiple_of(c * CHUNK, CHUNK)
        rows = pl.ds(r0, CHUNK)

        def seg(name, w):
            return pm_ref[rows, PM_OFF[name]:PM_OFF[name] + w]

        def finish_gated(col, norm_ref, gate_name):
            def finish(o):
                o = _group_rms(o, DV) * norm_ref[...] * _silu(seg(gate_name, BRANCH_W).astype(F32))
                out_ref[rows, col * BRANCH_W:(col + 1) * BRANCH_W] = o.astype(BF16)
            return finish

        def gla_prologue():
            return seg('gq', QK_W) * (DK ** -0.5), seg('gk', QK_W), seg('gv', BRANCH_W), g2a_ref[rows, :]
            yield

        def hgrn_prologue():
            return seg('hq', QK_W), kb_ref[rows, :], seg('hi', BRANCH_W), g2b_ref[rows, :]
            yield

        def ssd_prologue():
            cur = seg('sxbc', SSM_XBC)
            xc_ref[par, CONV_TAIL:CONV_TAIL + CHUNK, :] = cur
            taps = _dot(shift_ref[...], xc_ref[par])
            xc_ref[1 - par, 0:CONV_TAIL, :] = cur[CHUNK - CONV_TAIL:CHUNK, :]
            yield
            acc = jnp.broadcast_to(convb_ref[...], (CHUNK, SSM_XBC))
            for j in range(SSM_CONV):
                acc = acc + taps[j * CHUNK:(j + 1) * CHUNK] * convw_ref[j:j + 1, :]
            xbc = _silu(acc)
            return (xbc[:, 0:BRANCH_W], dtx_ref[rows, :], xbc[:, BRANCH_W:BRANCH_W + 128].astype(BF16),
                    xbc[:, BRANCH_W + 128:].astype(BF16))

        def ssd_finish(y):
            y = y * _silu(seg('sz', BRANCH_W).astype(F32))
            y = _group_rms(y, BRANCH_W // SSM_GROUPS) * snorm_ref[...]
            out_ref[rows, 2 * BRANCH_W:3 * BRANCH_W] = y.astype(BF16)

        def ret_prologue():
            return qrot_ref[rows, :], krot_ref[rows, :], seg('rv', BRANCH_W)
            yield

        return [
            _gla_branch(gla_prologue, finish_gated(0, gnorm_ref, 'gg'), amat, vmask_ref, j256,
                        sa_ref, sva_ref, kbda_ref.at[par], kbdta_ref.at[par], par, fast),
            _gla_branch(hgrn_prologue, finish_gated(1, hnorm_ref, 'hg'), amat, vmask_ref, j256,
                        sb_ref, svb_ref, kbdb_ref.at[par], kbdtb_ref.at[par], par, fast),
            _ssd_branch(ssd_prologue, ssd_finish, a2_row, dsk_ref[...], tri3_ref, triones_ref, ntrit_ref, caus_ref,
                        gmask_ref, ss_ref, ssbd_ref, xbd_ref.at[par], bbd_ref.at[par], par),
            _ret_branch(ret_prologue, finish_gated(3, rnorm_ref, 'rg'), dm_ref, qdec_ref, kdec_ref, cdec,
                        sd_ref, svd_ref, kbdd_ref.at[par], par),
        ]

    def delayed(gen, n):
        for _ in range(n):
            yield
        yield from gen

    def run_chunks(fast):
        n_pipe = PIPE_CHUNKS if fast else 1

        def group_body(p, carry):
            gens = []
            for u in range(n_pipe):
                c = p * n_pipe + u
                par = u % 2 if n_pipe % 2 == 0 else lax.rem(c, 2)
                gens += [delayed(g, PIPE_SKEW * u) for g in chunk_branches(c, fast, par)]
            _round_robin(gens)
            return carry

        lax.fori_loop(0, tile // (CHUNK * n_pipe), group_body, 0)

    @pl.when(safe)
    def _():
        run_chunks(True)

    @pl.when(jnp.logical_not(safe))
    def _():
        run_chunks(False)


def _const_spec(a):
    nd = a.ndim
    return pl.BlockSpec(a.shape, lambda b, i: (0,) * nd)


def _mixer(pm, pf, cos_t, sin_t, params, consts, *, batch, seq, layer, tile, cdec):
    n_t = seq // tile
    row_map = lambda b, i: (b * n_t + i, 0)
    small = list(params) + list(consts)
    in_specs = [
        pl.BlockSpec((tile, PM_W), row_map),
        pl.BlockSpec((tile, PF_W), row_map),
        pl.BlockSpec((tile, QK_W), lambda b, i: (i, 0)),
        pl.BlockSpec((tile, QK_W), lambda b, i: (i, 0)),
    ] + [_const_spec(a) for a in small]
    state = lambda: [pltpu.VMEM((QK_W, DV), F32), pltpu.VMEM((2, 2 * QK_W, BRANCH_W), BF16)]
    return pl.pallas_call(
        functools.partial(_mixer_kernel, layer=layer, tile=tile, cdec=cdec),
        grid=(batch, n_t),
        in_specs=in_specs,
        out_specs=pl.BlockSpec((tile, 4 * BRANCH_W), row_map),
        out_shape=jax.ShapeDtypeStruct((batch * seq, 4 * BRANCH_W), BF16),
        scratch_shapes=state() + state() + state() + [
            pltpu.VMEM((SSM_GROUPS * SSM_N, BRANCH_W), F32), pltpu.VMEM((2, SSM_GROUPS * SSM_N, BRANCH_W), BF16),
            pltpu.VMEM((2, N_LEVELS * QK_W, QK_W), BF16), pltpu.VMEM((2, N_LEVELS * QK_W, QK_W), BF16),
            pltpu.VMEM((2, QK_W, QK_W), BF16),
            pltpu.VMEM((2, SSM_HEADS * CHUNK, BRANCH_W), BF16), pltpu.VMEM((2, SSM_HEADS * CHUNK, SSM_GROUPS * SSM_N), BF16),
            pltpu.VMEM((2, CONV_TAIL + CHUNK, SSM_XBC), BF16),
            pltpu.VMEM((tile, QK_W), BF16), pltpu.VMEM((tile, QK_W), BF16),
            pltpu.VMEM((2, N_LEVELS * QK_W, QK_W), BF16), pltpu.VMEM((2, N_LEVELS * QK_W, QK_W), BF16),
            pltpu.VMEM((tile, QK_W), F32), pltpu.VMEM((tile, BRANCH_W), F32),
            pltpu.VMEM((tile, QK_W), F32), pltpu.VMEM((tile, QK_W), BF16),
        ],
        compiler_params=pltpu.CompilerParams(
            dimension_semantics=("arbitrary", "arbitrary"),
            vmem_limit_bytes=VMEM_LIMIT),
        name="mixer",
    )(pm, pf, cos_t, sin_t, *small)


def _merge_kernel(o_ref, mga_ref, mgb_ref, x_ref, wup_ref, wout_ref, nw_ref, *out_refs, final):
    merged = None
    for n in range(4):
        up = _dot(o_ref[:, n * BRANCH_W:(n + 1) * BRANCH_W], wup_ref[n])
        mg_ref = mga_ref if n < 2 else mgb_ref
        gate = _sigmoid(mg_ref[:, (n % 2) * D_MODEL:(n % 2 + 1) * D_MODEL].astype(F32))
        merged = gate * up if merged is None else merged + gate * up
    y = x_ref[...] + _dot(merged.astype(BF16), wout_ref[...])
    yn = _rms_rows(y, nw_ref[...])
    if final:
        out_refs[0][...] = yn
    else:
        out_refs[0][...] = y
        out_refs[1][...] = yn.astype(BF16)


def _merge(outs, proj, x2, wup, wout, nw, *, tm, final):
    m = x2.shape[0]
    row = lambda i: (i, 0)
    if final:
        out_specs = [pl.BlockSpec((tm, D_MODEL), row)]
        out_shape = [jax.ShapeDtypeStruct((m, D_MODEL), F32)]
    else:
        out_specs = [pl.BlockSpec((tm, D_MODEL), row), pl.BlockSpec((tm, D_MODEL), row)]
        out_shape = [jax.ShapeDtypeStruct((m, D_MODEL), F32), jax.ShapeDtypeStruct((m, D_MODEL), BF16)]
    return pl.pallas_call(
        functools.partial(_merge_kernel, final=final),
        grid=(m // tm,),
        in_specs=[
            pl.BlockSpec((tm, 4 * BRANCH_W), row),
            pl.BlockSpec((tm, MG_HALF), lambda i: (i, MG_OFF // MG_HALF)),
            pl.BlockSpec((tm, MG_HALF), lambda i: (i, MG_OFF // MG_HALF + 1)),
            pl.BlockSpec((tm, D_MODEL), row),
            pl.BlockSpec((4, BRANCH_W, D_MODEL), lambda i: (0, 0, 0)),
            pl.BlockSpec((D_MODEL, D_MODEL), lambda i: (0, 0)),
            pl.BlockSpec((1, D_MODEL), lambda i: (0, 0)),
        ],
        out_specs=out_specs,
        out_shape=out_shape,
        compiler_params=pltpu.CompilerParams(
            dimension_semantics=("arbitrary",),
            vmem_limit_bytes=VMEM_LIMIT),
        name="merge",
    )(outs, proj, proj, x2, wup, wout, nw)


def _tile_rows(n, pref):
    t = min(pref, n)
    assert n % t == 0
    return t


def kernel(x, norm_w, w_in, gla_w_lr, gla_b_lr, gla_norm, hg_lb_logits, hg_norm, ssm_conv_w, ssm_conv_b,
           ssm_dt_bias, ssm_a_log, ssm_d, ssm_norm, ret_norm, w_up, w_out, final_norm):
    batch, seq, d = x.shape
    depth = norm_w.shape[0]
    assert d == D_MODEL and seq % CHUNK == 0
    m = batch * seq
    tile = _tile_rows(seq, 512)
    assert tile % (CHUNK * PIPE_CHUNKS) == 0 and (tile // CHUNK) % 2 == 0 and PIPE_SKEW >= 1
    tm_proj = _tile_rows(m, 2048)
    tm_merge = _tile_rows(m, 512)

    w_in_r = _regroup_w_in(w_in)
    w_up_b = w_up.astype(BF16)
    w_out_b = w_out.astype(BF16)
    wlr = jnp.zeros((depth, 128, QK_W), F32).at[:, :GLA_RANK, :].set(gla_w_lr).astype(BF16)

    inv = 1.0 / (ROPE_BASE ** (jnp.arange(0, DK, 2, dtype=F32) / DK))
    ang = jnp.arange(seq, dtype=F32)[:, None] * inv[None, :]
    cos_t = jnp.tile(jnp.cos(ang), (1, 2 * N_HEADS))
    sin_t = jnp.tile(jnp.concatenate([-jnp.sin(ang), jnp.sin(ang)], axis=1), (1, N_HEADS))

    log_gamma = jnp.log(1.0 - 2.0 ** (-5.0 - jnp.arange(N_HEADS, dtype=F32)))
    lg_k = jnp.repeat(log_gamma, DK)[None, :]
    tpos = jnp.arange(CHUNK, dtype=F32)[:, None]
    spos = jnp.tile(jnp.arange(CHUNK, dtype=F32), N_HEADS)[None, :]
    dm = jnp.exp(lg_k * jnp.abs(tpos - spos))
    qdec = jnp.exp(lg_k * (tpos + 1.0)).astype(BF16)
    kdec = jnp.exp(lg_k * (CHUNK - 1.0 - tpos)).astype(BF16)
    cdec_np = np.exp(np.log(1.0 - 2.0 ** (-5.0 - np.arange(N_HEADS))) * CHUNK)
    cdec = tuple(float(c) for c in cdec_np)

    tri = np.tril(np.ones((CHUNK, CHUNK), np.float32))
    s_of_lane = np.tile(np.arange(CHUNK), SSM_HEADS)[None, :]
    srow = np.arange(CHUNK)[:, None]
    dtexp = np.zeros((128, BRANCH_W), np.float32)
    dtexp[np.arange(BRANCH_W) // SSM_P, np.arange(BRANCH_W)] = 1.0
    heads_per_group = SSM_HEADS // SSM_GROUPS
    gmask = (np.arange(SSM_GROUPS * SSM_N)[:, None] // SSM_N
             == np.arange(BRANCH_W)[None, :] // (SSM_P * heads_per_group)).astype(np.float32)
    consts = [
        jnp.asarray(_level_matrices(), BF16),
        jnp.asarray(_level_masks(), F32),
        jnp.asarray(np.kron(np.eye(N_HEADS, dtype=np.float32), np.ones((DK, CHUNK), np.float32)), BF16),
        dm, qdec, kdec,
        jnp.asarray(np.concatenate([tri, tri, tri], axis=1), BF16),
        jnp.asarray(np.concatenate([tri, tri, tri, np.ones((CHUNK, 3 * CHUNK), np.float32)], axis=1), BF16),
        jnp.asarray(-(srow <= s_of_lane).astype(np.float32), BF16),
        jnp.asarray((srow >= s_of_lane).astype(np.float32)),
        jnp.asarray(gmask),
        jnp.asarray(np.concatenate([dtexp, dtexp, dtexp], axis=0), BF16),
        jnp.asarray((np.arange(QK_W)[None, :] % DK < DK // 2).astype(np.float32)),
        jnp.asarray(_conv_shift_matrix(), BF16),
    ]

    x2 = x.reshape(m, D_MODEL)
    h = _norm(x2, norm_w[0][None, :], tm_proj)
    for l in range(depth):
        proj, pf = _inproj(h, w_in_r[l], tm_proj)
        params = [
            wlr[l], gla_b_lr[l][None, :], jnp.tile(gla_norm[l], N_HEADS)[None, :],
            hg_lb_logits, jnp.tile(hg_norm[l], N_HEADS)[None, :],
            ssm_conv_w[l], ssm_conv_b[l][None, :],
            jnp.repeat(ssm_dt_bias[l], SSM_P)[None, :], jnp.repeat(ssm_a_log[l], SSM_P)[None, :],
            jnp.repeat(ssm_d[l], SSM_P)[None, :], ssm_norm[l][None, :],
            jnp.tile(ret_norm[l], N_HEADS)[None, :],
        ]
        outs = _mixer(proj, pf, cos_t, sin_t, params, consts, batch=batch, seq=seq, layer=l, tile=tile, cdec=cdec)
        if l == depth - 1:
            (x2,) = _merge(outs, proj, x2, w_up_b[l], w_out_b[l], final_norm[None, :], tm=tm_merge, final=True)
        else:
            x2, h = _merge(outs, proj, x2, w_up_b[l], w_out_b[l], norm_w[l + 1][None, :], tm=tm_merge, final=False)
    return x2.reshape(batch, seq, D_MODEL)
```

```python
import functools
import math

import numpy as np
import jax
import jax.numpy as jnp
from jax import lax
from jax.experimental import pallas as pl
from jax.experimental.pallas import tpu as pltpu

F32 = jnp.float32
BF16 = jnp.bfloat16

D_MODEL = 1024
CHUNK = 64
EPS = 1e-6
LB_FLOOR = 1e-30
BRANCH_W = 512
N_HEADS = 4
DK = 64
DV = 128
QK_W = N_HEADS * DK
GLA_RANK = 16
GLA_TAU = 16.0
SSM_HEADS = 8
SSM_P = 64
SSM_GROUPS = 2
SSM_N = 64
SSM_CONV = 4
SSM_XBC = BRANCH_W + 2 * SSM_GROUPS * SSM_N
ROPE_BASE = 10000.0
LEVELS = (32, 16, 8, 4, 2, 1)
N_LEVELS = len(LEVELS)
LOG2E = math.log2(math.e)
CONV_TAIL = 16
PIPE_CHUNKS = 2
PIPE_SKEW = 2
SAFE_LOG2 = 100.0

_REF_SEGS = (('gq', 256), ('gk', 256), ('gv', 512), ('glr', 16), ('gg', 512),
             ('hq', 256), ('hf', 256), ('hi', 512), ('hg', 512),
             ('sz', 512), ('sxbc', 768), ('sdt', 8),
             ('rq', 256), ('rk', 256), ('rv', 512), ('rg', 512), ('mg', 4096))
_PM_SEGS = ('gq', 'gk', 'gv', 'gg', 'hq', 'hi', 'hg', 'sz', 'sxbc', 'rq', 'rk', 'rv', 'rg')
_PF_SLOTS = (('hf', 256), ('glr', 128), ('sdt', 128))
PM_W = 5632
PF_W = 512
MG_W = 4096
N_PROJ = PM_W + PF_W + MG_W
PROJ_TN = 1024
PF_BLOCK = (PM_W + PF_W) // PROJ_TN - 1
MG_OFF = PM_W + PF_W
MG_HALF = MG_W // 2
VMEM_LIMIT = 48 * 1024 * 1024


def _ref_offsets():
    off, out = 0, {}
    for name, w in _REF_SEGS:
        out[name] = (off, w)
        off += w
    return out


def _pm_offsets():
    ref = _ref_offsets()
    off, out = 0, {}
    for name in _PM_SEGS:
        out[name] = off
        off += ref[name][1]
    assert off == PM_W
    return out


PM_OFF = _pm_offsets()
PF_OFF = {'hf': 0, 'glr': 256, 'sdt': 384}


def _regroup_plan():
    ref = _ref_offsets()
    runs = []
    for name in _PM_SEGS:
        o, w = ref[name]
        if runs and runs[-1][0] + runs[-1][2] == o:
            runs[-1][2] += w
            runs[-1][3] += w
        else:
            runs.append([o, PM_OFF[name], w, w])
    for name, slot in _PF_SLOTS:
        runs.append([ref[name][0], PM_W + PF_OFF[name], ref[name][1], slot])
    runs.append([ref['mg'][0], PM_W + PF_W, MG_W, MG_W])
    return [tuple(r) for r in runs]


def _regroup_kernel(w_ref, o_ref):
    for src, dst, w, slot in _regroup_plan():
        if slot > w:
            o_ref[0, :, dst:dst + slot] = jnp.zeros((o_ref.shape[1], slot), BF16)
        o_ref[0, :, dst:dst + w] = w_ref[0, :, src:src + w].astype(BF16)


def _regroup_w_in(w_in):
    depth, d, n_in = w_in.shape
    rows = 256
    return pl.pallas_call(
        _regroup_kernel,
        grid=(depth, d // rows),
        in_specs=[pl.BlockSpec((1, rows, n_in), lambda l, r: (l, r, 0))],
        out_specs=pl.BlockSpec((1, rows, N_PROJ), lambda l, r: (l, r, 0)),
        out_shape=jax.ShapeDtypeStruct((depth, d, N_PROJ), BF16),
        compiler_params=pltpu.CompilerParams(
            dimension_semantics=("arbitrary", "arbitrary"),
            vmem_limit_bytes=VMEM_LIMIT),
        name="regroup",
    )(w_in)


def _level_matrices():
    t = np.arange(CHUNK)[:, None]
    s = np.arange(CHUNK)[None, :]
    blocks = [(s <= t), (s > t)]
    for m in LEVELS:
        r = (t // (2 * m)) * (2 * m) + m - 1
        blocks.append(np.where(t > r, (s > r) & (s <= t), (s > t) & (s <= r)))
    a = np.concatenate(blocks, axis=0).astype(np.float32)
    return np.concatenate([a, a, a], axis=1)


def _level_masks():
    t = np.arange(CHUNK)[:, None]
    s = np.tile(np.arange(CHUNK), N_HEADS)[None, :]
    out = []
    for m in LEVELS:
        out.append(((t // (2 * m)) == (s // (2 * m))) & ((t % (2 * m)) >= m) & ((s % (2 * m)) < m))
    out.append(t == s)
    out.append(t >= s)
    return np.concatenate(out, axis=0).astype(np.float32)


def _conv_shift_matrix():
    sh = np.zeros((SSM_CONV * CHUNK, CONV_TAIL + CHUNK), np.float32)
    for j in range(SSM_CONV):
        sh[j * CHUNK + np.arange(CHUNK), CONV_TAIL - (SSM_CONV - 1) + j + np.arange(CHUNK)] = 1.0
    return sh


def _split3(x):
    hi = x.astype(BF16)
    r = x - hi.astype(F32)
    mid = r.astype(BF16)
    lo = (r - mid.astype(F32)).astype(BF16)
    return hi, mid, lo


def _dot(a, b):
    return jnp.dot(a, b, preferred_element_type=F32)


def _dot_nt(a, b):
    return lax.dot_general(a, b, (((1,), (1,)), ((), ())), preferred_element_type=F32)


def _dot_tn(a, b):
    return lax.dot_general(a, b, (((0,), (0,)), ((), ())), preferred_element_type=F32)


def _sigmoid(x):
    return 0.5 * jnp.tanh(0.5 * x) + 0.5


def _silu(x):
    hx = 0.5 * x
    return hx * jnp.tanh(hx) + hx


def _log_sigmoid(x):
    return jnp.minimum(x, 0.0) - jnp.log(1.0 + jnp.exp(-jnp.abs(x)))


def _softplus(x):
    return jnp.maximum(x, 0.0) + jnp.log(1.0 + jnp.exp(-jnp.abs(x)))


def _group_rms(o, width):
    parts = []
    for h in range(o.shape[-1] // width):
        oh = o[:, h * width:(h + 1) * width]
        ms = jnp.mean(oh * oh, axis=-1, keepdims=True)
        parts.append(oh * lax.rsqrt(ms + EPS))
    return jnp.concatenate(parts, axis=1)


def _store_blockdiag(dst_ref, x, n_blocks, lane_w, lane_of_block=None):
    for b in range(n_blocks):
        g = b if lane_of_block is None else lane_of_block(b)
        dst_ref[b * CHUNK:(b + 1) * CHUNK, g * lane_w:(g + 1) * lane_w] = x[:, g * lane_w:(g + 1) * lane_w]


def _rms_rows(x, w):
    ms = jnp.mean(x * x, axis=-1, keepdims=True)
    return (x * lax.rsqrt(ms + EPS)) * w


def _norm_kernel(x_ref, nw_ref, h_ref):
    h_ref[...] = _rms_rows(x_ref[...], nw_ref[...]).astype(BF16)


def _norm(x2, nw, tm):
    m = x2.shape[0]
    return pl.pallas_call(
        _norm_kernel,
        grid=(m // tm,),
        in_specs=[pl.BlockSpec((tm, D_MODEL), lambda i: (i, 0)), pl.BlockSpec((1, D_MODEL), lambda i: (0, 0))],
        out_specs=pl.BlockSpec((tm, D_MODEL), lambda i: (i, 0)),
        out_shape=jax.ShapeDtypeStruct((m, D_MODEL), BF16),
        compiler_params=pltpu.CompilerParams(dimension_semantics=("arbitrary",)),
        name="norm",
    )(x2, nw)


def _inproj_kernel(h_ref, w_ref, proj_ref, pf_ref):
    res = _dot(h_ref[...], w_ref[...])
    proj_ref[...] = res.astype(BF16)

    @pl.when(pl.program_id(1) == PF_BLOCK)
    def _():
        pf_ref[...] = res[:, PROJ_TN - PF_W:]


def _inproj(h, w, tm):
    m = h.shape[0]
    return pl.pallas_call(
        _inproj_kernel,
        grid=(m // tm, N_PROJ // PROJ_TN),
        in_specs=[
            pl.BlockSpec((tm, D_MODEL), lambda i, j: (i, 0)),
            pl.BlockSpec((D_MODEL, PROJ_TN), lambda i, j: (0, j)),
        ],
        out_specs=[
            pl.BlockSpec((tm, PROJ_TN), lambda i, j: (i, j)),
            pl.BlockSpec((tm, PF_W), lambda i, j: (i, 0)),
        ],
        out_shape=[
            jax.ShapeDtypeStruct((m, N_PROJ), BF16),
            jax.ShapeDtypeStruct((m, PF_W), F32),
        ],
        compiler_params=pltpu.CompilerParams(
            dimension_semantics=("arbitrary", "arbitrary"),
            vmem_limit_bytes=VMEM_LIMIT),
        name="inproj",
    )(h, w)


def _gla_branch(prologue, finish, amat, vmask_ref, j256, s_ref, sv_ref, kbd_ref, kbdt_ref, par, fast):
    q_bf, k_bf, v_bf, g2 = yield from prologue()
    hi, mid, lo = _split3(g2)
    n_rows = 3 * CHUNK if fast else amat.shape[0]
    d = _dot(amat[0:n_rows], jnp.concatenate([hi, mid, lo], axis=0))
    yield
    e = jnp.exp2(d)
    e_bf = e.astype(BF16)
    qe = q_bf * e_bf[0:CHUNK]
    if fast:
        half = CHUNK // 2
        d_mid = d[2 * CHUNK:3 * CHUNK]
        near = e_bf[2 * CHUNK:3 * CHUNK]
        far = jnp.exp2(-d_mid).astype(BF16)
        qs = q_bf * jnp.concatenate([far[0:half], near[half:CHUNK]], axis=0)
        ks = k_bf * jnp.concatenate([near[0:half], far[half:CHUNK]], axis=0)
        blank = jnp.zeros((half, QK_W), BF16)
        lhs = jnp.concatenate([jnp.concatenate([blank, qs[half:CHUNK]], axis=0),
                               jnp.concatenate([qs[0:half], blank], axis=0)], axis=1)
        _store_blockdiag(kbd_ref.at[0:QK_W], ks, N_HEADS, DK)
        _store_blockdiag(kbd_ref.at[QK_W:2 * QK_W], jnp.concatenate([ks[0:half], blank], axis=0), N_HEADS, DK)
        kbdt_ref[0:QK_W, :] = kbd_ref[0:QK_W, :].T
        kbdt_ref[QK_W:2 * QK_W, :] = kbd_ref[QK_W:2 * QK_W, :].T
        sc = _dot(lhs, kbdt_ref[0:2 * QK_W, :])
        yield
        scores = sc * vmask_ref[(N_LEVELS + 1) * CHUNK:(N_LEVELS + 2) * CHUNK, :]
    else:
        scores = _dot(q_bf * k_bf, j256)
        yield
        scores = scores * vmask_ref[N_LEVELS * CHUNK:(N_LEVELS + 1) * CHUNK, :]
        pending = []
        for li in range(N_LEVELS):
            el = e_bf[(2 + li) * CHUNK:(3 + li) * CHUNK]
            lv = slice(li * QK_W, (li + 1) * QK_W)
            _store_blockdiag(kbd_ref.at[lv], k_bf * el, N_HEADS, DK)
            kbdt_ref[lv, :] = kbd_ref[lv, :].T
            pending.append((li, _dot(q_bf * el, kbdt_ref[lv, :])))
            if li % 2 == 1:
                yield
                for lj, sc in pending:
                    scores = scores + sc * vmask_ref[lj * CHUNK:(lj + 1) * CHUNK, :]
                pending = []
    _store_blockdiag(sv_ref.at[par, QK_W:2 * QK_W], v_bf, N_HEADS, DV)
    o = _dot(jnp.concatenate([qe, scores.astype(BF16)], axis=1), sv_ref[par])
    upd = _dot_tn(k_bf * e_bf[CHUNK:2 * CHUNK], v_bf)
    dcol = jnp.broadcast_to(e[CHUNK - 1:CHUNK, :], (DV, QK_W)).T
    yield
    for h in range(N_HEADS):
        r = slice(h * DK, (h + 1) * DK)
        c = slice(h * DV, (h + 1) * DV)
        new = s_ref[r, :] * dcol[r, :] + upd[r, c]
        s_ref[r, :] = new
        sv_ref[1 - par, r, c] = new.astype(BF16)
    finish(o)


def _ret_branch(prologue, finish, dm_ref, qdec_ref, kdec_ref, cdec, s_ref, sv_ref, kbd_ref, par):
    q_bf, k_bf, v_bf = yield from prologue()
    _store_blockdiag(kbd_ref, k_bf, N_HEADS, DK)
    scores = _dot_nt(q_bf, kbd_ref[...])
    _store_blockdiag(sv_ref.at[par, QK_W:2 * QK_W], v_bf, N_HEADS, DV)
    yield
    o = _dot(jnp.concatenate([q_bf * qdec_ref[...], (scores * dm_ref[...]).astype(BF16)], axis=1), sv_ref[par])
    upd = _dot_tn(k_bf * kdec_ref[...], v_bf)
    yield
    for h in range(N_HEADS):
        r = slice(h * DK, (h + 1) * DK)
        c = slice(h * DV, (h + 1) * DV)
        new = s_ref[r, :] * cdec[h] + upd[r, c]
        s_ref[r, :] = new
        sv_ref[1 - par, r, c] = new.astype(BF16)
    finish(o)


def _ssd_branch(prologue, finish, a2_row, d_row, tri3_ref, triones_ref, ntrit_ref, caus_ref, gmask_ref,
                ss_ref, ssbd_ref, xbd_ref, bbd_ref, par):
    heads_per_group = SSM_HEADS // SSM_GROUPS
    xs, dtx, bm_bf, cm_bf = yield from prologue()
    adt = dtx * a2_row
    hi, mid, lo = _split3(adt)
    ntrit = ntrit_ref[...]
    acol = _dot(tri3_ref[...], jnp.concatenate([hi, mid, lo], axis=0))
    aseg = _dot(triones_ref[...], jnp.concatenate([hi, mid, lo, hi * ntrit, mid * ntrit, lo * ntrit], axis=0))
    _store_blockdiag(bbd_ref, bm_bf, SSM_HEADS, SSM_N, lambda h: h // heads_per_group)
    cb = _dot_nt(cm_bf, bbd_ref[...])
    yoff = _dot(cm_bf, ssbd_ref[par])
    xdt = xs * dtx
    _store_blockdiag(xbd_ref, xdt.astype(BF16), SSM_HEADS, SSM_P)
    yield
    decay = jnp.exp2(jnp.minimum(aseg, 0.0)) * caus_ref[...]
    y = _dot((cb * decay).astype(BF16), xbd_ref[...])
    alast = acol[CHUNK - 1:CHUNK, :]
    upd = _dot_tn(bm_bf, (xdt * jnp.exp2(alast - acol)).astype(BF16))
    yield
    y = (jnp.exp2(acol) * yoff + d_row * xs) + y
    new = ss_ref[...] * jnp.exp2(alast) + upd * gmask_ref[...]
    ss_ref[...] = new
    ssbd_ref[1 - par] = new.astype(BF16)
    finish(y)


def _round_robin(gens):
    live = list(gens)
    while live:
        still = []
        for g in live:
            try:
                next(g)
                still.append(g)
            except StopIteration:
                pass
        live = still


def _mixer_kernel(pm_ref, pf_ref, cos_ref, sin_ref,
                  wlr_ref, blr_ref, gnorm_ref, lbl_ref, hnorm_ref,
                  convw_ref, convb_ref, dtb_ref, alog_ref, dsk_ref, snorm_ref, rnorm_ref,
                  amat_ref, vmask_ref, j256_ref, dm_ref, qdec_ref, kdec_ref,
                  tri3_ref, triones_ref, ntrit_ref, caus_ref, gmask_ref, dtexp_ref, lanelo_ref, shift_ref,
                  out_ref,
                  sa_ref, sva_ref, sb_ref, svb_ref, sd_ref, svd_ref, ss_ref, ssbd_ref,
                  kbda_ref, kbdb_ref, kbdd_ref, xbd_ref, bbd_ref, xc_ref,
                  qrot_ref, krot_ref, kbdta_ref, kbdtb_ref, g2a_ref, dtx_ref, g2b_ref, kb_ref,
                  *, layer, tile, cdec):
    i = pl.program_id(1)

    @pl.when(i == 0)
    def _():
        for r in (sa_ref, sva_ref, sb_ref, svb_ref, sd_ref, svd_ref, ss_ref, ssbd_ref,
                  kbda_ref, kbdb_ref, kbdd_ref, xbd_ref, bbd_ref):
            r[...] = jnp.zeros(r.shape, r.dtype)
        xc_ref[0, 0:CONV_TAIL, :] = jnp.zeros((CONV_TAIL, SSM_XBC), xc_ref.dtype)

    lbl = lbl_ref[...]
    ex = jnp.exp(lbl - jnp.max(lbl, axis=0, keepdims=True))
    lb = jnp.zeros((1, QK_W), F32)
    for l in range(layer):
        lb = lb + ex[l:l + 1, :]
    lb_floor = jnp.maximum(lb / jnp.sum(ex, axis=0, keepdims=True), LB_FLOOR)
    log_lb = jnp.log(lb_floor)

    amat = amat_ref[...]
    j256 = j256_ref[...]
    a2_row = -jnp.exp(alog_ref[...]) * LOG2E

    lr = pf_ref[:, PF_OFF['glr']:PF_OFF['glr'] + 128].astype(BF16)
    g2a_ref[...] = _log_sigmoid(_dot(lr, wlr_ref[...]) + blr_ref[...]) * (LOG2E / GLA_TAU)
    dth, dtm, dtl = _split3(pf_ref[:, PF_OFF['sdt']:PF_OFF['sdt'] + 128])
    dtx_ref[...] = _softplus(_dot(jnp.concatenate([dth, dtm, dtl], axis=1), dtexp_ref[...]) + dtb_ref[...])

    z = pf_ref[:, PF_OFF['hf']:PF_OFF['hf'] + QK_W]
    la = _log_sigmoid(z)
    lsn = la - z
    g2b_ref[...] = (jnp.maximum(la, log_lb + lsn) + jnp.log(1.0 + jnp.exp(-jnp.abs(z - log_lb)))) * LOG2E
    kb_ref[...] = ((1.0 - lb_floor) * jnp.exp(lsn)).astype(BF16)

    half = CHUNK // 2
    worst = None
    for g_ref in (g2a_ref, g2b_ref):
        for c in range(tile // half):
            s = jnp.sum(g_ref[c * half:(c + 1) * half, :], axis=0, keepdims=True)
            worst = s if worst is None else jnp.minimum(worst, s)
    safe = jnp.min(worst) >= -SAFE_LOG2

    lanelo = lanelo_ref[...]
    for c in range(tile // CHUNK):
        rws = slice(c * CHUNK, (c + 1) * CHUNK)
        cos = cos_ref[rws, :]
        sin = sin_ref[rws, :]
        for name, dst, scale in (('rq', qrot_ref, 1.0), ('rk', krot_ref, DK ** -0.5)):
            t = pm_ref[rws, PM_OFF[name]:PM_OFF[name] + QK_W].astype(F32)
            swapped = jnp.where(lanelo > 0.5, pltpu.roll(t, QK_W - DK // 2, 1), pltpu.roll(t, DK // 2, 1))
            dst[rws, :] = ((t * cos + swapped * sin) * scale).astype(BF16)

    def chunk_branches(c, fast, par):
        r0 = pl.multiple_of(c * CHUNK, CHUNK)
        rows = pl.ds(r0, CHUNK)

        def seg(name, w):
            return pm_ref[rows, PM_OFF[name]:PM_OFF[name] + w]

        def finish_gated(col, norm_ref, gate_name):
            def finish(o):
                o = _group_rms(o, DV) * norm_ref[...] * _silu(seg(gate_name, BRANCH_W).astype(F32))
                out_ref[rows, col * BRANCH_W:(col + 1) * BRANCH_W] = o.astype(BF16)
            return finish

        def gla_prologue():
            return seg('gq', QK_W) * (DK ** -0.5), seg('gk', QK_W), seg('gv', BRANCH_W), g2a_ref[rows, :]
            yield

        def hgrn_prologue():
            return seg('hq', QK_W), kb_ref[rows, :], seg('hi', BRANCH_W), g2b_ref[rows, :]
            yield

        def ssd_prologue():
            cur = seg('sxbc', SSM_XBC)
            xc_ref[par, CONV_TAIL:CONV_TAIL + CHUNK, :] = cur
            taps = _dot(shift_ref[...], xc_ref[par])
            xc_ref[1 - par, 0:CONV_TAIL, :] = cur[CHUNK - CONV_TAIL:CHUNK, :]
            yield
            acc = jnp.broadcast_to(convb_ref[...], (CHUNK, SSM_XBC))
            for j in range(SSM_CONV):
                acc = acc + taps[j * CHUNK:(j + 1) * CHUNK] * convw_ref[j:j + 1, :]
            xbc = _silu(acc)
            return (xbc[:, 0:BRANCH_W], dtx_ref[rows, :], xbc[:, BRANCH_W:BRANCH_W + 128].astype(BF16),
                    xbc[:, BRANCH_W + 128:].astype(BF16))

        def ssd_finish(y):
            y = y * _silu(seg('sz', BRANCH_W).astype(F32))
            y = _group_rms(y, BRANCH_W // SSM_GROUPS) * snorm_ref[...]
            out_ref[rows, 2 * BRANCH_W:3 * BRANCH_W] = y.astype(BF16)

        def ret_prologue():
            return qrot_ref[rows, :], krot_ref[rows, :], seg('rv', BRANCH_W)
            yield

        return [
            _gla_branch(gla_prologue, finish_gated(0, gnorm_ref, 'gg'), amat, vmask_ref, j256,
                        sa_ref, sva_ref, kbda_ref.at[par], kbdta_ref.at[par], par, fast),
            _gla_branch(hgrn_prologue, finish_gated(1, hnorm_ref, 'hg'), amat, vmask_ref, j256,
                        sb_ref, svb_ref, kbdb_ref.at[par], kbdtb_ref.at[par], par, fast),
            _ssd_branch(ssd_prologue, ssd_finish, a2_row, dsk_ref[...], tri3_ref, triones_ref, ntrit_ref, caus_ref,
                        gmask_ref, ss_ref, ssbd_ref, xbd_ref.at[par], bbd_ref.at[par], par),
            _ret_branch(ret_prologue, finish_gated(3, rnorm_ref, 'rg'), dm_ref, qdec_ref, kdec_ref, cdec,
                        sd_ref, svd_ref, kbdd_ref.at[par], par),
        ]

    def delayed(gen, n):
        for _ in range(n):
            yield
        yield from gen

    def run_chunks(fast):
        n_pipe = PIPE_CHUNKS if fast else 1

        def group_body(p, carry):
            gens = []
            for u in range(n_pipe):
                c = p * n_pipe + u
                par = u % 2 if n_pipe % 2 == 0 else lax.rem(c, 2)
                gens += [delayed(g, PIPE_SKEW * u) for g in chunk_branches(c, fast, par)]
            _round_robin(gens)
            return carry

        lax.fori_loop(0, tile // (CHUNK * n_pipe), group_body, 0)

    @pl.when(safe)
    def _():
        run_chunks(True)

    @pl.when(jnp.logical_not(safe))
    def _():
        run_chunks(False)


def _const_spec(a):
    nd = a.ndim
    return pl.BlockSpec(a.shape, lambda b, i: (0,) * nd)


def _mixer(pm, pf, cos_t, sin_t, params, consts, *, batch, seq, layer, tile, cdec):
    n_t = seq // tile
    row_map = lambda b, i: (b * n_t + i, 0)
    small = list(params) + list(consts)
    in_specs = [
        pl.BlockSpec((tile, PM_W), row_map),
        pl.BlockSpec((tile, PF_W), row_map),
        pl.BlockSpec((tile, QK_W), lambda b, i: (i, 0)),
        pl.BlockSpec((tile, QK_W), lambda b, i: (i, 0)),
    ] + [_const_spec(a) for a in small]
    state = lambda: [pltpu.VMEM((QK_W, DV), F32), pltpu.VMEM((2, 2 * QK_W, BRANCH_W), BF16)]
    return pl.pallas_call(
        functools.partial(_mixer_kernel, layer=layer, tile=tile, cdec=cdec),
        grid=(batch, n_t),
        in_specs=in_specs,
        out_specs=pl.BlockSpec((tile, 4 * BRANCH_W), row_map),
        out_shape=jax.ShapeDtypeStruct((batch * seq, 4 * BRANCH_W), BF16),
        scratch_shapes=state() + state() + state() + [
            pltpu.VMEM((SSM_GROUPS * SSM_N, BRANCH_W), F32), pltpu.VMEM((2, SSM_GROUPS * SSM_N, BRANCH_W), BF16),
            pltpu.VMEM((2, N_LEVELS * QK_W, QK_W), BF16), pltpu.VMEM((2, N_LEVELS * QK_W, QK_W), BF16),
            pltpu.VMEM((2, QK_W, QK_W), BF16),
            pltpu.VMEM((2, SSM_HEADS * CHUNK, BRANCH_W), BF16), pltpu.VMEM((2, SSM_HEADS * CHUNK, SSM_GROUPS * SSM_N), BF16),
            pltpu.VMEM((2, CONV_TAIL + CHUNK, SSM_XBC), BF16),
            pltpu.VMEM((tile, QK_W), BF16), pltpu.VMEM((tile, QK_W), BF16),
            pltpu.VMEM((2, N_LEVELS * QK_W, QK_W), BF16), pltpu.VMEM((2, N_LEVELS * QK_W, QK_W), BF16),
            pltpu.VMEM((tile, QK_W), F32), pltpu.VMEM((tile, BRANCH_W), F32),
            pltpu.VMEM((tile, QK_W), F32), pltpu.VMEM((tile, QK_W), BF16),
        ],
        compiler_params=pltpu.CompilerParams(
            dimension_semantics=("arbitrary", "arbitrary"),
            vmem_limit_bytes=VMEM_LIMIT),
        name="mixer",
    )(pm, pf, cos_t, sin_t, *small)


def _merge_kernel(o_ref, mga_ref, mgb_ref, x_ref, wup_ref, wout_ref, nw_ref, *out_refs, final):
    merged = None
    for n in range(4):
        up = _dot(o_ref[:, n * BRANCH_W:(n + 1) * BRANCH_W], wup_ref[n])
        mg_ref = mga_ref if n < 2 else mgb_ref
        gate = _sigmoid(mg_ref[:, (n % 2) * D_MODEL:(n % 2 + 1) * D_MODEL].astype(F32))
        merged = gate * up if merged is None else merged + gate * up
    y = x_ref[...] + _dot(merged.astype(BF16), wout_ref[...])
    yn = _rms_rows(y, nw_ref[...])
    if final:
        out_refs[0][...] = yn
    else:
        out_refs[0][...] = y
        out_refs[1][...] = yn.astype(BF16)


def _merge(outs, proj, x2, wup, wout, nw, *, tm, final):
    m = x2.shape[0]
    row = lambda i: (i, 0)
    if final:
        out_specs = [pl.BlockSpec((tm, D_MODEL), row)]
        out_shape = [jax.ShapeDtypeStruct((m, D_MODEL), F32)]
    else:
        out_specs = [pl.BlockSpec((tm, D_MODEL), row), pl.BlockSpec((tm, D_MODEL), row)]
        out_shape = [jax.ShapeDtypeStruct((m, D_MODEL), F32), jax.ShapeDtypeStruct((m, D_MODEL), BF16)]
    return pl.pallas_call(
        functools.partial(_merge_kernel, final=final),
        grid=(m // tm,),
        in_specs=[
            pl.BlockSpec((tm, 4 * BRANCH_W), row),
            pl.BlockSpec((tm, MG_HALF), lambda i: (i, MG_OFF // MG_HALF)),
            pl.BlockSpec((tm, MG_HALF), lambda i: (i, MG_OFF // MG_HALF + 1)),
            pl.BlockSpec((tm, D_MODEL), row),
            pl.BlockSpec((4, BRANCH_W, D_MODEL), lambda i: (0, 0, 0)),
            pl.BlockSpec((D_MODEL, D_MODEL), lambda i: (0, 0)),
            pl.BlockSpec((1, D_MODEL), lambda i: (0, 0)),
        ],
        out_specs=out_specs,
        out_shape=out_shape,
        compiler_params=pltpu.CompilerParams(
            dimension_semantics=("arbitrary",),
            vmem_limit_bytes=VMEM_LIMIT),
        name="merge",
    )(outs, proj, proj, x2, wup, wout, nw)


def _tile_rows(n, pref):
    t = min(pref, n)
    assert n % t == 0
    return t


def kernel(x, norm_w, w_in, gla_w_lr, gla_b_lr, gla_norm, hg_lb_logits, hg_norm, ssm_conv_w, ssm_conv_b,
           ssm_dt_bias, ssm_a_log, ssm_d, ssm_norm, ret_norm, w_up, w_out, final_norm):
    batch, seq, d = x.shape
    depth = norm_w.shape[0]
    assert d == D_MODEL and seq % CHUNK == 0
    m = batch * seq
    tile = _tile_rows(seq, 512)
    assert tile % (CHUNK * PIPE_CHUNKS) == 0 and (tile // CHUNK) % 2 == 0 and PIPE_SKEW >= 1
    tm_proj = _tile_rows(m, 2048)
    tm_merge = _tile_rows(m, 512)

    w_in_r = _regroup_w_in(w_in)
    w_up_b = w_up.astype(BF16)
    w_out_b = w_out.astype(BF16)
    wlr = jnp.zeros((depth, 128, QK_W), F32).at[:, :GLA_RANK, :].set(gla_w_lr).astype(BF16)

    inv = 1.0 / (ROPE_BASE ** (jnp.arange(0, DK, 2, dtype=F32) / DK))
    ang = jnp.arange(seq, dtype=F32)[:, None] * inv[None, :]
    cos_t = jnp.tile(jnp.cos(ang), (1, 2 * N_HEADS))
    sin_t = jnp.tile(jnp.concatenate([-jnp.sin(ang), jnp.sin(ang)], axis=1), (1, N_HEADS))

    log_gamma = jnp.log(1.0 - 2.0 ** (-5.0 - jnp.arange(N_HEADS, dtype=F32)))
    lg_k = jnp.repeat(log_gamma, DK)[None, :]
    tpos = jnp.arange(CHUNK, dtype=F32)[:, None]
    spos = jnp.tile(jnp.arange(CHUNK, dtype=F32), N_HEADS)[None, :]
    dm = jnp.exp(lg_k * jnp.abs(tpos - spos))
    qdec = jnp.exp(lg_k * (tpos + 1.0)).astype(BF16)
    kdec = jnp.exp(lg_k * (CHUNK - 1.0 - tpos)).astype(BF16)
    cdec_np = np.exp(np.log(1.0 - 2.0 ** (-5.0 - np.arange(N_HEADS))) * CHUNK)
    cdec = tuple(float(c) for c in cdec_np)

    tri = np.tril(np.ones((CHUNK, CHUNK), np.float32))
    s_of_lane = np.tile(np.arange(CHUNK), SSM_HEADS)[None, :]
    srow = np.arange(CHUNK)[:, None]
    dtexp = np.zeros((128, BRANCH_W), np.float32)
    dtexp[np.arange(BRANCH_W) // SSM_P, np.arange(BRANCH_W)] = 1.0
    heads_per_group = SSM_HEADS // SSM_GROUPS
    gmask = (np.arange(SSM_GROUPS * SSM_N)[:, None] // SSM_N
             == np.arange(BRANCH_W)[None, :] // (SSM_P * heads_per_group)).astype(np.float32)
    consts = [
        jnp.asarray(_level_matrices(), BF16),
        jnp.asarray(_level_masks(), F32),
        jnp.asarray(np.kron(np.eye(N_HEADS, dtype=np.float32), np.ones((DK, CHUNK), np.float32)), BF16),
        dm, qdec, kdec,
        jnp.asarray(np.concatenate([tri, tri, tri], axis=1), BF16),
        jnp.asarray(np.concatenate([tri, tri, tri, np.ones((CHUNK, 3 * CHUNK), np.float32)], axis=1), BF16),
        jnp.asarray(-(srow <= s_of_lane).astype(np.float32), BF16),
        jnp.asarray((srow >= s_of_lane).astype(np.float32)),
        jnp.asarray(gmask),
        jnp.asarray(np.concatenate([dtexp, dtexp, dtexp], axis=0), BF16),
        jnp.asarray((np.arange(QK_W)[None, :] % DK < DK // 2).astype(np.float32)),
        jnp.asarray(_conv_shift_matrix(), BF16),
    ]

    x2 = x.reshape(m, D_MODEL)
    h = _norm(x2, norm_w[0][None, :], tm_proj)
    for l in range(depth):
        proj, pf = _inproj(h, w_in_r[l], tm_proj)
        params = [
            wlr[l], gla_b_lr[l][None, :], jnp.tile(gla_norm[l], N_HEADS)[None, :],
            hg_lb_logits, jnp.tile(hg_norm[l], N_HEADS)[None, :],
            ssm_conv_w[l], ssm_conv_b[l][None, :],
            jnp.repeat(ssm_dt_bias[l], SSM_P)[None, :], jnp.repeat(ssm_a_log[l], SSM_P)[None, :],
            jnp.repeat(ssm_d[l], SSM_P)[None, :], ssm_norm[l][None, :],
            jnp.tile(ret_norm[l], N_HEADS)[None, :],
        ]
        outs = _mixer(proj, pf, cos_t, sin_t, params, consts, batch=batch, seq=seq, layer=l, tile=tile, cdec=cdec)
        if l == depth - 1:
            (x2,) = _merge(outs, proj, x2, w_up_b[l], w_out_b[l], final_norm[None, :], tm=tm_merge, final=True)
        else:
            x2, h = _merge(outs, proj, x2, w_up_b[l], w_out_b[l], norm_w[l + 1][None, :], tm=tm_merge, final=False)
    return x2.reshape(batch, seq, D_MODEL)
```

```python
import functools
import math

import numpy as np
import jax
import jax.numpy as jnp
from jax import lax
from jax.experimental import pallas as pl
from jax.experimental.pallas import tpu as pltpu

F32 = jnp.float32
BF16 = jnp.bfloat16

D_MODEL = 1024
CHUNK = 64
EPS = 1e-6
LB_FLOOR = 1e-30
BRANCH_W = 512
N_HEADS = 4
DK = 64
DV = 128
QK_W = N_HEADS * DK
GLA_RANK = 16
GLA_TAU = 16.0
SSM_HEADS = 8
SSM_P = 64
SSM_GROUPS = 2
SSM_N = 64
SSM_CONV = 4
SSM_XBC = BRANCH_W + 2 * SSM_GROUPS * SSM_N
ROPE_BASE = 10000.0
LEVELS = (32, 16, 8, 4, 2, 1)
N_LEVELS = len(LEVELS)
LOG2E = math.log2(math.e)
CONV_TAIL = 16
PIPE_CHUNKS = 2
PIPE_SKEW = 2
SAFE_LOG2 = 100.0

_REF_SEGS = (('gq', 256), ('gk', 256), ('gv', 512), ('glr', 16), ('gg', 512),
             ('hq', 256), ('hf', 256), ('hi', 512), ('hg', 512),
             ('sz', 512), ('sxbc', 768), ('sdt', 8),
             ('rq', 256), ('rk', 256), ('rv', 512), ('rg', 512), ('mg', 4096))
_PM_SEGS = ('gq', 'gk', 'gv', 'gg', 'hq', 'hi', 'hg', 'sz', 'sxbc', 'rq', 'rk', 'rv', 'rg')
_PF_SLOTS = (('hf', 256), ('glr', 128), ('sdt', 128))
PM_W = 5632
PF_W = 512
MG_W = 4096
N_PROJ = PM_W + PF_W + MG_W
PROJ_TN = 1024
PF_BLOCK = (PM_W + PF_W) // PROJ_TN - 1
MG_OFF = PM_W + PF_W
MG_HALF = MG_W // 2
VMEM_LIMIT = 48 * 1024 * 1024


def _ref_offsets():
    off, out = 0, {}
    for name, w in _REF_SEGS:
        out[name] = (off, w)
        off += w
    return out


def _pm_offsets():
    ref = _ref_offsets()
    off, out = 0, {}
    for name in _PM_SEGS:
        out[name] = off
        off += ref[name][1]
    assert off == PM_W
    return out


PM_OFF = _pm_offsets()
PF_OFF = {'hf': 0, 'glr': 256, 'sdt': 384}


def _regroup_plan():
    ref = _ref_offsets()
    runs = []
    for name in _PM_SEGS:
        o, w = ref[name]
        if runs and runs[-1][0] + runs[-1][2] == o:
            runs[-1][2] += w
            runs[-1][3] += w
        else:
            runs.append([o, PM_OFF[name], w, w])
    for name, slot in _PF_SLOTS:
        runs.append([ref[name][0], PM_W + PF_OFF[name], ref[name][1], slot])
    runs.append([ref['mg'][0], PM_W + PF_W, MG_W, MG_W])
    return [tuple(r) for r in runs]


def _regroup_kernel(w_ref, o_ref):
    for src, dst, w, slot in _regroup_plan():
        if slot > w:
            o_ref[0, :, dst:dst + slot] = jnp.zeros((o_ref.shape[1], slot), BF16)
        o_ref[0, :, dst:dst + w] = w_ref[0, :, src:src + w].astype(BF16)


def _regroup_w_in(w_in):
    depth, d, n_in = w_in.shape
    rows = 256
    return pl.pallas_call(
        _regroup_kernel,
        grid=(depth, d // rows),
        in_specs=[pl.BlockSpec((1, rows, n_in), lambda l, r: (l, r, 0))],
        out_specs=pl.BlockSpec((1, rows, N_PROJ), lambda l, r: (l, r, 0)),
        out_shape=jax.ShapeDtypeStruct((depth, d, N_PROJ), BF16),
        compiler_params=pltpu.CompilerParams(
            dimension_semantics=("arbitrary", "arbitrary"),
            vmem_limit_bytes=VMEM_LIMIT),
        name="regroup",
    )(w_in)


def _level_matrices():
    t = np.arange(CHUNK)[:, None]
    s = np.arange(CHUNK)[None, :]
    blocks = [(s <= t), (s > t)]
    for m in LEVELS:
        r = (t // (2 * m)) * (2 * m) + m - 1
        blocks.append(np.where(t > r, (s > r) & (s <= t), (s > t) & (s <= r)))
    a = np.concatenate(blocks, axis=0).astype(np.float32)
    return np.concatenate([a, a, a], axis=1)


def _level_masks():
    t = np.arange(CHUNK)[:, None]
    s = np.tile(np.arange(CHUNK), N_HEADS)[None, :]
    out = []
    for m in LEVELS:
        out.append(((t // (2 * m)) == (s // (2 * m))) & ((t % (2 * m)) >= m) & ((s % (2 * m)) < m))
    out.append(t == s)
    out.append(t >= s)
    return np.concatenate(out, axis=0).astype(np.float32)


def _conv_shift_matrix():
    sh = np.zeros((SSM_CONV * CHUNK, CONV_TAIL + CHUNK), np.float32)
    for j in range(SSM_CONV):
        sh[j * CHUNK + np.arange(CHUNK), CONV_TAIL - (SSM_CONV - 1) + j + np.arange(CHUNK)] = 1.0
    return sh


def _split3(x):
    hi = x.astype(BF16)
    r = x - hi.astype(F32)
    mid = r.astype(BF16)
    lo = (r - mid.astype(F32)).astype(BF16)
    return hi, mid, lo


def _dot(a, b):
    return jnp.dot(a, b, preferred_element_type=F32)


def _dot_nt(a, b):
    return lax.dot_general(a, b, (((1,), (1,)), ((), ())), preferred_element_type=F32)


def _dot_tn(a, b):
    return lax.dot_general(a, b, (((0,), (0,)), ((), ())), preferred_element_type=F32)


def _sigmoid(x):
    return 0.5 * jnp.tanh(0.5 * x) + 0.5


def _silu(x):
    hx = 0.5 * x
    return hx * jnp.tanh(hx) + hx


def _log_sigmoid(x):
    return jnp.minimum(x, 0.0) - jnp.log(1.0 + jnp.exp(-jnp.abs(x)))


def _softplus(x):
    return jnp.maximum(x, 0.0) + jnp.log(1.0 + jnp.exp(-jnp.abs(x)))


def _group_rms(o, width):
    parts = []
    for h in range(o.shape[-1] // width):
        oh = o[:, h * width:(h + 1) * width]
        ms = jnp.mean(oh * oh, axis=-1, keepdims=True)
        parts.append(oh * lax.rsqrt(ms + EPS))
    return jnp.concatenate(parts, axis=1)


def _store_blockdiag(dst_ref, x, n_blocks, lane_w, lane_of_block=None):
    for b in range(n_blocks):
        g = b if lane_of_block is None else lane_of_block(b)
        dst_ref[b * CHUNK:(b + 1) * CHUNK, g * lane_w:(g + 1) * lane_w] = x[:, g * lane_w:(g + 1) * lane_w]


def _rms_rows(x, w):
    ms = jnp.mean(x * x, axis=-1, keepdims=True)
    return (x * lax.rsqrt(ms + EPS)) * w


def _norm_kernel(x_ref, nw_ref, h_ref):
    h_ref[...] = _rms_rows(x_ref[...], nw_ref[...]).astype(BF16)


def _norm(x2, nw, tm):
    m = x2.shape[0]
    return pl.pallas_call(
        _norm_kernel,
        grid=(m // tm,),
        in_specs=[pl.BlockSpec((tm, D_MODEL), lambda i: (i, 0)), pl.BlockSpec((1, D_MODEL), lambda i: (0, 0))],
        out_specs=pl.BlockSpec((tm, D_MODEL), lambda i: (i, 0)),
        out_shape=jax.ShapeDtypeStruct((m, D_MODEL), BF16),
        compiler_params=pltpu.CompilerParams(dimension_semantics=("arbitrary",)),
        name="norm",
    )(x2, nw)


def _inproj_kernel(h_ref, w_ref, proj_ref, pf_ref):
    res = _dot(h_ref[...], w_ref[...])
    proj_ref[...] = res.astype(BF16)

    @pl.when(pl.program_id(1) == PF_BLOCK)
    def _():
        pf_ref[...] = res[:, PROJ_TN - PF_W:]


def _inproj(h, w, tm):
    m = h.shape[0]
    return pl.pallas_call(
        _inproj_kernel,
        grid=(m // tm, N_PROJ // PROJ_TN),
        in_specs=[
            pl.BlockSpec((tm, D_MODEL), lambda i, j: (i, 0)),
            pl.BlockSpec((D_MODEL, PROJ_TN), lambda i, j: (0, j)),
        ],
        out_specs=[
            pl.BlockSpec((tm, PROJ_TN), lambda i, j: (i, j)),
            pl.BlockSpec((tm, PF_W), lambda i, j: (i, 0)),
        ],
        out_shape=[
            jax.ShapeDtypeStruct((m, N_PROJ), BF16),
            jax.ShapeDtypeStruct((m, PF_W), F32),
        ],
        compiler_params=pltpu.CompilerParams(
            dimension_semantics=("arbitrary", "arbitrary"),
            vmem_limit_bytes=VMEM_LIMIT),
        name="inproj",
    )(h, w)


def _gla_branch(prologue, finish, amat, vmask_ref, j256, s_ref, sv_ref, kbd_ref, kbdt_ref, par, fast):
    q_bf, k_bf, v_bf, g2 = yield from prologue()
    hi, mid, lo = _split3(g2)
    n_rows = 3 * CHUNK if fast else amat.shape[0]
    d = _dot(amat[0:n_rows], jnp.concatenate([hi, mid, lo], axis=0))
    yield
    e = jnp.exp2(d)
    e_bf = e.astype(BF16)
    qe = q_bf * e_bf[0:CHUNK]
    if fast:
        half = CHUNK // 2
        d_mid = d[2 * CHUNK:3 * CHUNK]
        near = e_bf[2 * CHUNK:3 * CHUNK]
        far = jnp.exp2(-d_mid).astype(BF16)
        qs = q_bf * jnp.concatenate([far[0:half], near[half:CHUNK]], axis=0)
        ks = k_bf * jnp.concatenate([near[0:half], far[half:CHUNK]], axis=0)
        blank = jnp.zeros((half, QK_W), BF16)
        lhs = jnp.concatenate([jnp.concatenate([blank, qs[half:CHUNK]], axis=0),
                               jnp.concatenate([qs[0:half], blank], axis=0)], axis=1)
        _store_blockdiag(kbd_ref.at[0:QK_W], ks, N_HEADS, DK)
        _store_blockdiag(kbd_ref.at[QK_W:2 * QK_W], jnp.concatenate([ks[0:half], blank], axis=0), N_HEADS, DK)
        kbdt_ref[0:QK_W, :] = kbd_ref[0:QK_W, :].T
        kbdt_ref[QK_W:2 * QK_W, :] = kbd_ref[QK_W:2 * QK_W, :].T
        sc = _dot(lhs, kbdt_ref[0:2 * QK_W, :])
        yield
        scores = sc * vmask_ref[(N_LEVELS + 1) * CHUNK:(N_LEVELS + 2) * CHUNK, :]
    else:
        scores = _dot(q_bf * k_bf, j256)
        yield
        scores = scores * vmask_ref[N_LEVELS * CHUNK:(N_LEVELS + 1) * CHUNK, :]
        pending = []
        for li in range(N_LEVELS):
            el = e_bf[(2 + li) * CHUNK:(3 + li) * CHUNK]
            lv = slice(li * QK_W, (li + 1) * QK_W)
            _store_blockdiag(kbd_ref.at[lv], k_bf * el, N_HEADS, DK)
            kbdt_ref[lv, :] = kbd_ref[lv, :].T
            pending.append((li, _dot(q_bf * el, kbdt_ref[lv, :])))
            if li % 2 == 1:
                yield
                for lj, sc in pending:
                    scores = scores + sc * vmask_ref[lj * CHUNK:(lj + 1) * CHUNK, :]
                pending = []
    _store_blockdiag(sv_ref.at[par, QK_W:2 * QK_W], v_bf, N_HEADS, DV)
    o = _dot(jnp.concatenate([qe, scores.astype(BF16)], axis=1), sv_ref[par])
    upd = _dot_tn(k_bf * e_bf[CHUNK:2 * CHUNK], v_bf)
    dcol = jnp.broadcast_to(e[CHUNK - 1:CHUNK, :], (DV, QK_W)).T
    yield
    for h in range(N_HEADS):
        r = slice(h * DK, (h + 1) * DK)
        c = slice(h * DV, (h + 1) * DV)
        new = s_ref[r, :] * dcol[r, :] + upd[r, c]
        s_ref[r, :] = new
        sv_ref[1 - par, r, c] = new.astype(BF16)
    finish(o)


def _ret_branch(prologue, finish, dm_ref, qdec_ref, kdec_ref, cdec, s_ref, sv_ref, kbd_ref, par):
    q_bf, k_bf, v_bf = yield from prologue()
    _store_blockdiag(kbd_ref, k_bf, N_HEADS, DK)
    scores = _dot_nt(q_bf, kbd_ref[...])
    _store_blockdiag(sv_ref.at[par, QK_W:2 * QK_W], v_bf, N_HEADS, DV)
    yield
    o = _dot(jnp.concatenate([q_bf * qdec_ref[...], (scores * dm_ref[...]).astype(BF16)], axis=1), sv_ref[par])
    upd = _dot_tn(k_bf * kdec_ref[...], v_bf)
    yield
    for h in range(N_HEADS):
        r = slice(h * DK, (h + 1) * DK)
        c = slice(h * DV, (h + 1) * DV)
        new = s_ref[r, :] * cdec[h] + upd[r, c]
        s_ref[r, :] = new
        sv_ref[1 - par, r, c] = new.astype(BF16)
    finish(o)


def _ssd_branch(prologue, finish, a2_row, d_row, tri3_ref, triones_ref, ntrit_ref, caus_ref, gmask_ref,
                ss_ref, ssbd_ref, xbd_ref, bbd_ref, par):
    heads_per_group = SSM_HEADS // SSM_GROUPS
    xs, dtx, bm_bf, cm_bf = yield from prologue()
    adt = dtx * a2_row
    hi, mid, lo = _split3(adt)
    ntrit = ntrit_ref[...]
    acol = _dot(tri3_ref[...], jnp.concatenate([hi, mid, lo], axis=0))
    aseg = _dot(triones_ref[...], jnp.concatenate([hi, mid, lo, hi * ntrit, mid * ntrit, lo * ntrit], axis=0))
    _store_blockdiag(bbd_ref, bm_bf, SSM_HEADS, SSM_N, lambda h: h // heads_per_group)
    cb = _dot_nt(cm_bf, bbd_ref[...])
    yoff = _dot(cm_bf, ssbd_ref[par])
    xdt = xs * dtx
    _store_blockdiag(xbd_ref, xdt.astype(BF16), SSM_HEADS, SSM_P)
    yield
    decay = jnp.exp2(jnp.minimum(aseg, 0.0)) * caus_ref[...]
    y = _dot((cb * decay).astype(BF16), xbd_ref[...])
    alast = acol[CHUNK - 1:CHUNK, :]
    upd = _dot_tn(bm_bf, (xdt * jnp.exp2(alast - acol)).astype(BF16))
    yield
    y = (jnp.exp2(acol) * yoff + d_row * xs) + y
    new = ss_ref[...] * jnp.exp2(alast) + upd * gmask_ref[...]
    ss_ref[...] = new
    ssbd_ref[1 - par] = new.astype(BF16)
    finish(y)


def _round_robin(gens):
    live = list(gens)
    while live:
        still = []
        for g in live:
            try:
                next(g)
                still.append(g)
            except StopIteration:
                pass
        live = still


def _mixer_kernel(pm_ref, pf_ref, cos_ref, sin_ref,
                  wlr_ref, blr_ref, gnorm_ref, lbl_ref, hnorm_ref,
                  convw_ref, convb_ref, dtb_ref, alog_ref, dsk_ref, snorm_ref, rnorm_ref,
                  amat_ref, vmask_ref, j256_ref, dm_ref, qdec_ref, kdec_ref,
                  tri3_ref, triones_ref, ntrit_ref, caus_ref, gmask_ref, dtexp_ref, lanelo_ref, shift_ref,
                  out_ref,
                  sa_ref, sva_ref, sb_ref, svb_ref, sd_ref, svd_ref, ss_ref, ssbd_ref,
                  kbda_ref, kbdb_ref, kbdd_ref, xbd_ref, bbd_ref, xc_ref,
                  qrot_ref, krot_ref, kbdta_ref, kbdtb_ref, g2a_ref, dtx_ref, g2b_ref, kb_ref,
                  *, layer, tile, cdec):
    i = pl.program_id(1)

    @pl.when(i == 0)
    def _():
        for r in (sa_ref, sva_ref, sb_ref, svb_ref, sd_ref, svd_ref, ss_ref, ssbd_ref,
                  kbda_ref, kbdb_ref, kbdd_ref, xbd_ref, bbd_ref):
            r[...] = jnp.zeros(r.shape, r.dtype)
        xc_ref[0, 0:CONV_TAIL, :] = jnp.zeros((CONV_TAIL, SSM_XBC), xc_ref.dtype)

    lbl = lbl_ref[...]
    ex = jnp.exp(lbl - jnp.max(lbl, axis=0, keepdims=True))
    lb = jnp.zeros((1, QK_W), F32)
    for l in range(layer):
        lb = lb + ex[l:l + 1, :]
    lb_floor = jnp.maximum(lb / jnp.sum(ex, axis=0, keepdims=True), LB_FLOOR)
    log_lb = jnp.log(lb_floor)

    amat = amat_ref[...]
    j256 = j256_ref[...]
    a2_row = -jnp.exp(alog_ref[...]) * LOG2E

    lr = pf_ref[:, PF_OFF['glr']:PF_OFF['glr'] + 128].astype(BF16)
    g2a_ref[...] = _log_sigmoid(_dot(lr, wlr_ref[...]) + blr_ref[...]) * (LOG2E / GLA_TAU)
    dth, dtm, dtl = _split3(pf_ref[:, PF_OFF['sdt']:PF_OFF['sdt'] + 128])
    dtx_ref[...] = _softplus(_dot(jnp.concatenate([dth, dtm, dtl], axis=1), dtexp_ref[...]) + dtb_ref[...])

    z = pf_ref[:, PF_OFF['hf']:PF_OFF['hf'] + QK_W]
    la = _log_sigmoid(z)
    lsn = la - z
    g2b_ref[...] = (jnp.maximum(la, log_lb + lsn) + jnp.log(1.0 + jnp.exp(-jnp.abs(z - log_lb)))) * LOG2E
    kb_ref[...] = ((1.0 - lb_floor) * jnp.exp(lsn)).astype(BF16)

    half = CHUNK // 2
    worst = None
    for g_ref in (g2a_ref, g2b_ref):
        for c in range(tile // half):
            s = jnp.sum(g_ref[c * half:(c + 1) * half, :], axis=0, keepdims=True)
            worst = s if worst is None else jnp.minimum(worst, s)
    safe = jnp.min(worst) >= -SAFE_LOG2

    lanelo = lanelo_ref[...]
    for c in range(tile // CHUNK):
        rws = slice(c * CHUNK, (c + 1) * CHUNK)
        cos = cos_ref[rws, :]
        sin = sin_ref[rws, :]
        for name, dst, scale in (('rq', qrot_ref, 1.0), ('rk', krot_ref, DK ** -0.5)):
            t = pm_ref[rws, PM_OFF[name]:PM_OFF[name] + QK_W].astype(F32)
            swapped = jnp.where(lanelo > 0.5, pltpu.roll(t, QK_W - DK // 2, 1), pltpu.roll(t, DK // 2, 1))
            dst[rws, :] = ((t * cos + swapped * sin) * scale).astype(BF16)

    def chunk_branches(c, fast, par):
        r0 = pl.multiple_of(c * CHUNK, CHUNK)
        rows = pl.ds(r0, CHUNK)

        def seg(name, w):
            return pm_ref[rows, PM_OFF[name]:PM_OFF[name] + w]

        def finish_gated(col, norm_ref, gate_name):
            def finish(o):
                o = _group_rms(o, DV) * norm_ref[...] * _silu(seg(gate_name, BRANCH_W).astype(F32))
                out_ref[rows, col * BRANCH_W:(col + 1) * BRANCH_W] = o.astype(BF16)
            return finish

        def gla_prologue():
            return seg('gq', QK_W) * (DK ** -0.5), seg('gk', QK_W), seg('gv', BRANCH_W), g2a_ref[rows, :]
            yield

        def hgrn_prologue():
            return seg('hq', QK_W), kb_ref[rows, :], seg('hi', BRANCH_W), g2b_ref[rows, :]
            yield

        def ssd_prologue():
            cur = seg('sxbc', SSM_XBC)
            xc_ref[par, CONV_TAIL:CONV_TAIL + CHUNK, :] = cur
            taps = _dot(shift_ref[...], xc_ref[par])
            xc_ref[1 - par, 0:CONV_TAIL, :] = cur[CHUNK - CONV_TAIL:CHUNK, :]
            yield
            acc = jnp.broadcast_to(convb_ref[...], (CHUNK, SSM_XBC))
            for j in range(SSM_CONV):
                acc = acc + taps[j * CHUNK:(j + 1) * CHUNK] * convw_ref[j:j + 1, :]
            xbc = _silu(acc)
            return (xbc[:, 0:BRANCH_W], dtx_ref[rows, :], xbc[:, BRANCH_W:BRANCH_W + 128].astype(BF16),
                    xbc[:, BRANCH_W + 128:].astype(BF16))

        def ssd_finish(y):
            y = y * _silu(seg('sz', BRANCH_W).astype(F32))
            y = _group_rms(y, BRANCH_W // SSM_GROUPS) * snorm_ref[...]
            out_ref[rows, 2 * BRANCH_W:3 * BRANCH_W] = y.astype(BF16)

        def ret_prologue():
            return qrot_ref[rows, :], krot_ref[rows, :], seg('rv', BRANCH_W)
            yield

        return [
            _gla_branch(gla_prologue, finish_gated(0, gnorm_ref, 'gg'), amat, vmask_ref, j256,
                        sa_ref, sva_ref, kbda_ref.at[par], kbdta_ref.at[par], par, fast),
            _gla_branch(hgrn_prologue, finish_gated(1, hnorm_ref, 'hg'), amat, vmask_ref, j256,
                        sb_ref, svb_ref, kbdb_ref.at[par], kbdtb_ref.at[par], par, fast),
            _ssd_branch(ssd_prologue, ssd_finish, a2_row, dsk_ref[...], tri3_ref, triones_ref, ntrit_ref, caus_ref,
                        gmask_ref, ss_ref, ssbd_ref, xbd_ref.at[par], bbd_ref.at[par], par),
            _ret_branch(ret_prologue, finish_gated(3, rnorm_ref, 'rg'), dm_ref, qdec_ref, kdec_ref, cdec,
                        sd_ref, svd_ref, kbdd_ref.at[par], par),
        ]

    def delayed(gen, n):
        for _ in range(n):
            yield
        yield from gen

    def run_chunks(fast):
        n_pipe = PIPE_CHUNKS if fast else 1

        def group_body(p, carry):
            gens = []
            for u in range(n_pipe):
                c = p * n_pipe + u
                par = u % 2 if n_pipe % 2 == 0 else lax.rem(c, 2)
                gens += [delayed(g, PIPE_SKEW * u) for g in chunk_branches(c, fast, par)]
            _round_robin(gens)
            return carry

        lax.fori_loop(0, tile // (CHUNK * n_pipe), group_body, 0)

    @pl.when(safe)
    def _():
        run_chunks(True)

    @pl.when(jnp.logical_not(safe))
    def _():
        run_chunks(False)


def _const_spec(a):
    nd = a.ndim
    return pl.BlockSpec(a.shape, lambda b, i: (0,) * nd)


def _mixer(pm, pf, cos_t, sin_t, params, consts, *, batch, seq, layer, tile, cdec):
    n_t = seq // tile
    row_map = lambda b, i: (b * n_t + i, 0)
    small = list(params) + list(consts)
    in_specs = [
        pl.BlockSpec((tile, PM_W), row_map),
        pl.BlockSpec((tile, PF_W), row_map),
        pl.BlockSpec((tile, QK_W), lambda b, i: (i, 0)),
        pl.BlockSpec((tile, QK_W), lambda b, i: (i, 0)),
    ] + [_const_spec(a) for a in small]
    state = lambda: [pltpu.VMEM((QK_W, DV), F32), pltpu.VMEM((2, 2 * QK_W, BRANCH_W), BF16)]
    return pl.pallas_call(
        functools.partial(_mixer_kernel, layer=layer, tile=tile, cdec=cdec),
        grid=(batch, n_t),
        in_specs=in_specs,
        out_specs=pl.BlockSpec((tile, 4 * BRANCH_W), row_map),
        out_shape=jax.ShapeDtypeStruct((batch * seq, 4 * BRANCH_W), BF16),
        scratch_shapes=state() + state() + state() + [
            pltpu.VMEM((SSM_GROUPS * SSM_N, BRANCH_W), F32), pltpu.VMEM((2, SSM_GROUPS * SSM_N, BRANCH_W), BF16),
            pltpu.VMEM((2, N_LEVELS * QK_W, QK_W), BF16), pltpu.VMEM((2, N_LEVELS * QK_W, QK_W), BF16),
            pltpu.VMEM((2, QK_W, QK_W), BF16),
            pltpu.VMEM((2, SSM_HEADS * CHUNK, BRANCH_W), BF16), pltpu.VMEM((2, SSM_HEADS * CHUNK, SSM_GROUPS * SSM_N), BF16),
            pltpu.VMEM((2, CONV_TAIL + CHUNK, SSM_XBC), BF16),
            pltpu.VMEM((tile, QK_W), BF16), pltpu.VMEM((tile, QK_W), BF16),
            pltpu.VMEM((2, N_LEVELS * QK_W, QK_W), BF16), pltpu.VMEM((2, N_LEVELS * QK_W, QK_W), BF16),
            pltpu.VMEM((tile, QK_W), F32), pltpu.VMEM((tile, BRANCH_W), F32),
            pltpu.VMEM((tile, QK_W), F32), pltpu.VMEM((tile, QK_W), BF16),
        ],
        compiler_params=pltpu.CompilerParams(
            dimension_semantics=("arbitrary", "arbitrary"),
            vmem_limit_bytes=VMEM_LIMIT),
        name="mixer",
    )(pm, pf, cos_t, sin_t, *small)


def _merge_kernel(o_ref, mga_ref, mgb_ref, x_ref, wup_ref, wout_ref, nw_ref, *refs, final):
    out_refs, merged_ref = refs[:-1], refs[-1]
    half = D_MODEL // 2
    for ch in range(2):
        cols = slice(ch * half, (ch + 1) * half)
        merged = None
        for n in range(4):
            up = _dot(o_ref[:, n * BRANCH_W:(n + 1) * BRANCH_W], wup_ref[n, :, cols])
            mg_ref = mga_ref if n < 2 else mgb_ref
            lo = (n % 2) * D_MODEL + ch * half
            gate = _sigmoid(mg_ref[:, lo:lo + half].astype(F32))
            merged = gate * up if merged is None else merged + gate * up
        merged_ref[:, cols] = merged.astype(BF16)
    y = x_ref[...] + _dot(merged_ref[...], wout_ref[...])
    yn = _rms_rows(y, nw_ref[...])
    if final:
        out_refs[0][...] = yn
    else:
        out_refs[0][...] = y
        out_refs[1][...] = yn.astype(BF16)


def _merge(outs, proj, x2, wup, wout, nw, *, tm, final):
    m = x2.shape[0]
    row = lambda i: (i, 0)
    if final:
        out_specs = [pl.BlockSpec((tm, D_MODEL), row)]
        out_shape = [jax.ShapeDtypeStruct((m, D_MODEL), F32)]
    else:
        out_specs = [pl.BlockSpec((tm, D_MODEL), row), pl.BlockSpec((tm, D_MODEL), row)]
        out_shape = [jax.ShapeDtypeStruct((m, D_MODEL), F32), jax.ShapeDtypeStruct((m, D_MODEL), BF16)]
    return pl.pallas_call(
        functools.partial(_merge_kernel, final=final),
        grid=(m // tm,),
        in_specs=[
            pl.BlockSpec((tm, 4 * BRANCH_W), row),
            pl.BlockSpec((tm, MG_HALF), lambda i: (i, MG_OFF // MG_HALF)),
            pl.BlockSpec((tm, MG_HALF), lambda i: (i, MG_OFF // MG_HALF + 1)),
            pl.BlockSpec((tm, D_MODEL), row),
            pl.BlockSpec((4, BRANCH_W, D_MODEL), lambda i: (0, 0, 0)),
            pl.BlockSpec((D_MODEL, D_MODEL), lambda i: (0, 0)),
            pl.BlockSpec((1, D_MODEL), lambda i: (0, 0)),
        ],
        out_specs=out_specs,
        out_shape=out_shape,
        scratch_shapes=[pltpu.VMEM((tm, D_MODEL), BF16)],
        compiler_params=pltpu.CompilerParams(
            dimension_semantics=("arbitrary",),
            vmem_limit_bytes=VMEM_LIMIT),
        name="merge",
    )(outs, proj, proj, x2, wup, wout, nw)


def _tile_rows(n, pref):
    t = min(pref, n)
    assert n % t == 0
    return t


def kernel(x, norm_w, w_in, gla_w_lr, gla_b_lr, gla_norm, hg_lb_logits, hg_norm, ssm_conv_w, ssm_conv_b,
           ssm_dt_bias, ssm_a_log, ssm_d, ssm_norm, ret_norm, w_up, w_out, final_norm):
    batch, seq, d = x.shape
    depth = norm_w.shape[0]
    assert d == D_MODEL and seq % CHUNK == 0
    m = batch * seq
    tile = _tile_rows(seq, 512)
    assert tile % (CHUNK * PIPE_CHUNKS) == 0 and (tile // CHUNK) % 2 == 0 and PIPE_SKEW >= 1
    tm_proj = _tile_rows(m, 2048)
    tm_merge = _tile_rows(m, 512)

    w_in_r = _regroup_w_in(w_in)
    w_up_b = w_up.astype(BF16)
    w_out_b = w_out.astype(BF16)
    wlr = jnp.zeros((depth, 128, QK_W), F32).at[:, :GLA_RANK, :].set(gla_w_lr).astype(BF16)

    inv = 1.0 / (ROPE_BASE ** (jnp.arange(0, DK, 2, dtype=F32) / DK))
    ang = jnp.arange(seq, dtype=F32)[:, None] * inv[None, :]
    cos_t = jnp.tile(jnp.cos(ang), (1, 2 * N_HEADS))
    sin_t = jnp.tile(jnp.concatenate([-jnp.sin(ang), jnp.sin(ang)], axis=1), (1, N_HEADS))

    log_gamma = jnp.log(1.0 - 2.0 ** (-5.0 - jnp.arange(N_HEADS, dtype=F32)))
    lg_k = jnp.repeat(log_gamma, DK)[None, :]
    tpos = jnp.arange(CHUNK, dtype=F32)[:, None]
    spos = jnp.tile(jnp.arange(CHUNK, dtype=F32), N_HEADS)[None, :]
    dm = jnp.exp(lg_k * jnp.abs(tpos - spos))
    qdec = jnp.exp(lg_k * (tpos + 1.0)).astype(BF16)
    kdec = jnp.exp(lg_k * (CHUNK - 1.0 - tpos)).astype(BF16)
    cdec_np = np.exp(np.log(1.0 - 2.0 ** (-5.0 - np.arange(N_HEADS))) * CHUNK)
    cdec = tuple(float(c) for c in cdec_np)

    tri = np.tril(np.ones((CHUNK, CHUNK), np.float32))
    s_of_lane = np.tile(np.arange(CHUNK), SSM_HEADS)[None, :]
    srow = np.arange(CHUNK)[:, None]
    dtexp = np.zeros((128, BRANCH_W), np.float32)
    dtexp[np.arange(BRANCH_W) // SSM_P, np.arange(BRANCH_W)] = 1.0
    heads_per_group = SSM_HEADS // SSM_GROUPS
    gmask = (np.arange(SSM_GROUPS * SSM_N)[:, None] // SSM_N
             == np.arange(BRANCH_W)[None, :] // (SSM_P * heads_per_group)).astype(np.float32)
    consts = [
        jnp.asarray(_level_matrices(), BF16),
        jnp.asarray(_level_masks(), F32),
        jnp.asarray(np.kron(np.eye(N_HEADS, dtype=np.float32), np.ones((DK, CHUNK), np.float32)), BF16),
        dm, qdec, kdec,
        jnp.asarray(np.concatenate([tri, tri, tri], axis=1), BF16),
        jnp.asarray(np.concatenate([tri, tri, tri, np.ones((CHUNK, 3 * CHUNK), np.float32)], axis=1), BF16),
        jnp.asarray(-(srow <= s_of_lane).astype(np.float32), BF16),
        jnp.asarray((srow >= s_of_lane).astype(np.float32)),
        jnp.asarray(gmask),
        jnp.asarray(np.concatenate([dtexp, dtexp, dtexp], axis=0), BF16),
        jnp.asarray((np.arange(QK_W)[None, :] % DK < DK // 2).astype(np.float32)),
        jnp.asarray(_conv_shift_matrix(), BF16),
    ]

    x2 = x.reshape(m, D_MODEL)
    h = _norm(x2, norm_w[0][None, :], tm_proj)
    for l in range(depth):
        proj, pf = _inproj(h, w_in_r[l], tm_proj)
        params = [
            wlr[l], gla_b_lr[l][None, :], jnp.tile(gla_norm[l], N_HEADS)[None, :],
            hg_lb_logits, jnp.tile(hg_norm[l], N_HEADS)[None, :],
            ssm_conv_w[l], ssm_conv_b[l][None, :],
            jnp.repeat(ssm_dt_bias[l], SSM_P)[None, :], jnp.repeat(ssm_a_log[l], SSM_P)[None, :],
            jnp.repeat(ssm_d[l], SSM_P)[None, :], ssm_norm[l][None, :],
            jnp.tile(ret_norm[l], N_HEADS)[None, :],
        ]
        outs = _mixer(proj, pf, cos_t, sin_t, params, consts, batch=batch, seq=seq, layer=l, tile=tile, cdec=cdec)
        if l == depth - 1:
            (x2,) = _merge(outs, proj, x2, w_up_b[l], w_out_b[l], final_norm[None, :], tm=tm_merge, final=True)
        else:
            x2, h = _merge(outs, proj, x2, w_up_b[l], w_out_b[l], norm_w[l + 1][None, :], tm=tm_merge, final=False)
    return x2.reshape(batch, seq, D_MODEL)
```
